```python
import jax, jax.numpy as jnp
from jax import lax
import numpy as np

D_MODEL = 1024
BATCH = 16
SEQ = 4096
DEPTH = 4

N_MEM = 256
N_EVEN = (DEPTH + 1) // 2
N_ODD = DEPTH // 2
A_WIDTH = D_MODEL // 2
A_CONV = 3
B_WIDTH = D_MODEL
B_HEADS = 8
B_HEAD_DIM = B_WIDTH // B_HEADS
B_CONV = 4
LRU_C = 8.0
EV_IN = 3 * A_WIDTH + 2 * B_WIDTH
M_HEADS = 4
M_QK_DIM = D_MODEL // 8
M_V_DIM = D_MODEL // 4
M_QK = M_HEADS * M_QK_DIM
M_V = M_HEADS * M_V_DIM
OD_IN = 2 * M_QK + 2 * M_V + 2 * M_HEADS
M_CHUNK = 128
X_HEADS = 4
X_HEAD_DIM = D_MODEL // X_HEADS
N_EXPERTS = 32
TOP_K = 4
D_FF = D_MODEL
SWIGLU_LIMIT = 7.0
SWIGLU_ALPHA = 1.702
DN_ALPHA = (2 * DEPTH) ** 0.25
DN_BETA = (8 * DEPTH) ** -0.25
LN_EPS = 1e-5
RMS_EPS = 1e-6

kernel_name = 'hybrid_conv_rglru_mlstm_moe_deepnorm'


def layer_norm(x, g, b):
    xf = x.astype(jnp.float32)
    mu = jnp.mean(xf, axis=-1, keepdims=True)
    var = jnp.mean(jnp.square(xf - mu), axis=-1, keepdims=True)
    return ((xf - mu) * lax.rsqrt(var + LN_EPS) * g + b).astype(x.dtype)


def causal_conv(x, w):
    k = w.shape[0]
    return lax.conv_general_dilated(
        x, w[:, None, :], window_strides=(1,), padding=[(k - 1, 0)],
        dimension_numbers=('NWC', 'WIO', 'NWC'), feature_group_count=x.shape[-1])


def rg_lru(u, w_r, b_r, w_i, b_i, lam):
    bsz, s, _ = u.shape
    uh = u.reshape(bsz, s, B_HEADS, B_HEAD_DIM)
    r = jax.nn.sigmoid(jnp.einsum('bshd,hde->bshe', uh, w_r).reshape(bsz, s, B_WIDTH) + b_r)
    ig = jax.nn.sigmoid(jnp.einsum('bshd,hde->bshe', uh, w_i).reshape(bsz, s, B_WIDTH) + b_i)
    log_a = -LRU_C * r.astype(jnp.float32) * jax.nn.softplus(-lam.astype(jnp.float32))
    a = jnp.exp(log_a)
    inp = jnp.sqrt(-jnp.expm1(2.0 * log_a)) * (ig * u).astype(jnp.float32)

    def combine(left, right):
        a1, b1 = left
        a2, b2 = right
        return a1 * a2, a2 * b1 + b2

    _, h = lax.associative_scan(combine, (a, inp), axis=1)
    return h.astype(u.dtype)


def even_mixer(x, w_in, conv_a, conv_b, conv_b_bias, w_r, b_r, w_i, b_i, lam, w_out):
    z = x @ w_in
    a_b, a_c, a_x, b_u, b_g = jnp.split(
        z, [A_WIDTH, 2 * A_WIDTH, 3 * A_WIDTH, 3 * A_WIDTH + B_WIDTH], axis=-1)
    y_a = a_b * causal_conv(a_c * a_x, conv_a)
    u = causal_conv(b_u, conv_b) + conv_b_bias
    y_b = jax.nn.gelu(b_g) * rg_lru(u, w_r, b_r, w_i, b_i, lam)
    return jnp.concatenate([y_a, y_b], axis=-1) @ w_out


def to_chunks(t):
    bsz, s = t.shape[:2]
    t = t.reshape(bsz, s // M_CHUNK, M_CHUNK, *t.shape[2:])
    return jnp.moveaxis(jnp.moveaxis(t, 1, 0), 3, 2)


def mlstm_chunkwise(q, k, v, i_pre, log_f):
    bsz, s, nh, dk = q.shape
    dv = v.shape[-1]
    causal = jnp.tril(jnp.ones((M_CHUNK, M_CHUNK), dtype=bool))

    def step(carry, xs):
        c_st, n_st, m_st = carry
        qc, kc, vc, ic, fc = xs
        b = jnp.cumsum(fc, axis=-1)
        d = b[..., :, None] - b[..., None, :] + ic[..., None, :]
        d = jnp.where(causal, d, -jnp.inf)
        inter = b + m_st[..., None]
        m_t = jnp.maximum(inter, jnp.max(d, axis=-1))
        p = jnp.exp(d - m_t[..., None])
        w_inter = jnp.exp(inter - m_t)
        qk = jnp.einsum('bhtd,bhsd->bhts', qc, kc) * p
        num = jnp.einsum('bhts,bhsv->bhtv', qk, vc) + w_inter[..., None] * jnp.einsum('bhtd,bhdv->bhtv', qc, c_st)
        den = jnp.sum(qk, axis=-1) + w_inter * jnp.einsum('bhtd,bhd->bht', qc, n_st)
        h = num / jnp.maximum(jnp.abs(den), jnp.exp(-m_t))[..., None]
        b_end = b[..., -1]
        g = b_end[..., None] - b + ic
        m_new = jnp.maximum(b_end + m_st, jnp.max(g, axis=-1))
        wg = jnp.exp(g - m_new[..., None])
        decay = jnp.exp(b_end + m_st - m_new)
        c_new = decay[..., None, None] * c_st + jnp.einsum('bhs,bhsd,bhsv->bhdv', wg, kc, vc)
        n_new = decay[..., None] * n_st + jnp.einsum('bhs,bhsd->bhd', wg, kc)
        return (c_new, n_new, m_new), h

    init = (jnp.zeros((bsz, nh, dk, dv), jnp.float32),
            jnp.zeros((bsz, nh, dk), jnp.float32),
            jnp.zeros((bsz, nh), jnp.float32))
    xs = (to_chunks(q), to_chunks(k), to_chunks(v), to_chunks(i_pre), to_chunks(log_f))
    _, h = lax.scan(step, init, xs)
    h = jnp.moveaxis(jnp.moveaxis(h, 0, 1), 2, 3)
    return h.reshape(bsz, s, nh, dv)


def odd_mixer(x, w_in, b_gates, norm_g, w_out):
    bsz, s, _ = x.shape
    z = x @ w_in
    q, k, v, o, gates = jnp.split(z, [M_QK, 2 * M_QK, 2 * M_QK + M_V, 2 * M_QK + 2 * M_V], axis=-1)
    q = q.reshape(bsz, s, M_HEADS, M_QK_DIM).astype(jnp.float32) * (M_QK_DIM ** -0.5)
    k = k.reshape(bsz, s, M_HEADS, M_QK_DIM).astype(jnp.float32)
    v = v.reshape(bsz, s, M_HEADS, M_V_DIM).astype(jnp.float32)
    gates = gates.astype(jnp.float32) + b_gates.astype(jnp.float32)
    i_pre, f_pre = gates[..., :M_HEADS], gates[..., M_HEADS:]
    h = mlstm_chunkwise(q, k, v, i_pre, jax.nn.log_sigmoid(f_pre))
    h = h * lax.rsqrt(jnp.mean(jnp.square(h), axis=-1, keepdims=True) + RMS_EPS)
    h = h.reshape(bsz, s, M_V) * norm_g
    return (jax.nn.sigmoid(o) * h.astype(x.dtype)) @ w_out


def memory_attention(x, mem, wq, wk, wv, wo):
    bsz, s, _ = x.shape
    q = (x @ wq).reshape(bsz, s, X_HEADS, X_HEAD_DIM)
    k = (mem @ wk).reshape(bsz, -1, X_HEADS, X_HEAD_DIM)
    v = (mem @ wv).reshape(bsz, -1, X_HEADS, X_HEAD_DIM)
    sc = jnp.einsum('bshd,bmhd->bhsm', q, k).astype(jnp.float32) * (X_HEAD_DIM ** -0.5)
    p = jax.nn.softmax(sc, axis=-1).astype(x.dtype)
    o = jnp.einsum('bhsm,bmhd->bshd', p, v).reshape(bsz, s, D_MODEL)
    return o @ wo


def moe(x, w_router, b_router, w1, b1, w2, b2):
    bsz, s, _ = x.shape
    t = x.reshape(-1, D_MODEL)
    logits = (t @ w_router + b_router).astype(jnp.float32)
    top_val, top_idx = lax.top_k(logits, TOP_K)
    top_w = jax.nn.softmax(top_val, axis=-1)
    combine = jnp.sum(jax.nn.one_hot(top_idx, N_EXPERTS, dtype=jnp.float32) * top_w[..., None], axis=1)
    combine = combine.astype(x.dtype)
    y = jnp.zeros_like(t)
    for e in range(N_EXPERTS):
        gu = t @ w1[e] + b1[e]
        gate = jnp.minimum(gu[:, :D_FF], SWIGLU_LIMIT)
        up = jnp.clip(gu[:, D_FF:], -SWIGLU_LIMIT, SWIGLU_LIMIT)
        hid = (up + 1.0) * gate * jax.nn.sigmoid(SWIGLU_ALPHA * gate)
        y = y + combine[:, e:e + 1] * (hid @ w2[e] + b2[e])
    return y.reshape(bsz, s, D_MODEL)


def setup_inputs(seed: int = 0) -> dict:
    key = jax.random.key(seed)
    keys = iter(jax.random.split(key, 32))

    def nrm(shape, scale):
        return jax.random.normal(next(keys), shape, jnp.float32) * scale

    x = nrm((BATCH, SEQ, D_MODEL), 1.0)
    mem = nrm((BATCH, N_MEM, D_MODEL), 1.0)
    ev_w_in = nrm((N_EVEN, D_MODEL, EV_IN), D_MODEL ** -0.5)
    ev_conv_a = nrm((N_EVEN, A_CONV, A_WIDTH), A_CONV ** -0.5)
    ev_conv_b = nrm((N_EVEN, B_CONV, B_WIDTH), B_CONV ** -0.5)
    ev_conv_b_bias = nrm((N_EVEN, B_WIDTH), 0.01)
    ev_w_rgate = nrm((N_EVEN, B_HEADS, B_HEAD_DIM, B_HEAD_DIM), B_HEAD_DIM ** -0.5)
    ev_b_rgate = nrm((N_EVEN, B_WIDTH), 0.01)
    ev_w_igate = nrm((N_EVEN, B_HEADS, B_HEAD_DIM, B_HEAD_DIM), B_HEAD_DIM ** -0.5)
    ev_b_igate = nrm((N_EVEN, B_WIDTH), 0.01)
    a_pow = jax.random.uniform(next(keys), (N_EVEN, B_WIDTH), jnp.float32, 0.9, 0.999)
    a_base = a_pow ** (1.0 / LRU_C)
    ev_lambda = jnp.log(a_base) - jnp.log1p(-a_base)
    ev_w_out = nrm((N_EVEN, A_WIDTH + B_WIDTH, D_MODEL), (A_WIDTH + B_WIDTH) ** -0.5 * DN_BETA)
    od_w_in = nrm((N_ODD, D_MODEL, OD_IN), D_MODEL ** -0.5)
    f_bias = jnp.linspace(3.0, 6.0, M_HEADS, dtype=jnp.float32)
    od_b_gates = jnp.concatenate([nrm((N_ODD, M_HEADS), 0.1), f_bias + nrm((N_ODD, M_HEADS), 0.1)], axis=-1)
    od_norm_g = 1.0 + nrm((N_ODD, M_V), 0.02)
    od_w_out = nrm((N_ODD, M_V, D_MODEL), M_V ** -0.5 * DN_BETA)
    xa_wq = nrm((DEPTH, D_MODEL, D_MODEL), D_MODEL ** -0.5)
    xa_wk = nrm((DEPTH, D_MODEL, D_MODEL), D_MODEL ** -0.5)
    xa_wv = nrm((DEPTH, D_MODEL, D_MODEL), D_MODEL ** -0.5)
    xa_wo = nrm((DEPTH, D_MODEL, D_MODEL), D_MODEL ** -0.5 * DN_BETA)
    moe_w_router = nrm((DEPTH, D_MODEL, N_EXPERTS), D_MODEL ** -0.5)
    moe_b_router = nrm((DEPTH, N_EXPERTS), 0.01)
    moe_w1 = nrm((DEPTH, N_EXPERTS, D_MODEL, 2 * D_FF), D_MODEL ** -0.5)
    moe_b1 = nrm((DEPTH, N_EXPERTS, 2 * D_FF), 0.01)
    moe_w2 = nrm((DEPTH, N_EXPERTS, D_FF, D_MODEL), D_FF ** -0.5 * DN_BETA)
    moe_b2 = nrm((DEPTH, N_EXPERTS, D_MODEL), 0.01)
    ln_g = 1.0 + nrm((DEPTH, 3, D_MODEL), 0.02)
    ln_b = nrm((DEPTH, 3, D_MODEL), 0.01)
    return {'x': x, 'mem': mem,
            'ev_w_in': ev_w_in, 'ev_conv_a': ev_conv_a, 'ev_conv_b': ev_conv_b,
            'ev_conv_b_bias': ev_conv_b_bias, 'ev_w_rgate': ev_w_rgate, 'ev_b_rgate': ev_b_rgate,
            'ev_w_igate': ev_w_igate, 'ev_b_igate': ev_b_igate, 'ev_lambda': ev_lambda,
            'ev_w_out': ev_w_out,
            'od_w_in': od_w_in, 'od_b_gates': od_b_gates, 'od_norm_g': od_norm_g, 'od_w_out': od_w_out,
            'xa_wq': xa_wq, 'xa_wk': xa_wk, 'xa_wv': xa_wv, 'xa_wo': xa_wo,
            'moe_w_router': moe_w_router, 'moe_b_router': moe_b_router, 'moe_w1': moe_w1,
            'moe_b1': moe_b1, 'moe_w2': moe_w2, 'moe_b2': moe_b2,
            'ln_g': ln_g, 'ln_b': ln_b}


def reference(x, mem, ev_w_in, ev_conv_a, ev_conv_b, ev_conv_b_bias, ev_w_rgate, ev_b_rgate,
              ev_w_igate, ev_b_igate, ev_lambda, ev_w_out, od_w_in, od_b_gates, od_norm_g, od_w_out,
              xa_wq, xa_wk, xa_wv, xa_wo, moe_w_router, moe_b_router, moe_w1, moe_b1, moe_w2, moe_b2,
              ln_g, ln_b):
    for layer in range(DEPTH):
        j = layer // 2
        if layer % 2 == 0:
            mix = even_mixer(x, ev_w_in[j], ev_conv_a[j], ev_conv_b[j], ev_conv_b_bias[j],
                             ev_w_rgate[j], ev_b_rgate[j], ev_w_igate[j], ev_b_igate[j],
                             ev_lambda[j], ev_w_out[j])
        else:
            mix = odd_mixer(x, od_w_in[j], od_b_gates[j], od_norm_g[j], od_w_out[j])
        x = layer_norm(DN_ALPHA * x + mix, ln_g[layer, 0], ln_b[layer, 0])
        att = memory_attention(x, mem, xa_wq[layer], xa_wk[layer], xa_wv[layer], xa_wo[layer])
        x = layer_norm(DN_ALPHA * x + att, ln_g[layer, 1], ln_b[layer, 1])
        ffn = moe(x, moe_w_router[layer], moe_b_router[layer], moe_w1[layer], moe_b1[layer],
                  moe_w2[layer], moe_b2[layer])
        x = layer_norm(DN_ALPHA * x + ffn, ln_g[layer, 2], ln_b[layer, 2])
    return x
```

```python
import functools

import jax
import jax.numpy as jnp
from jax import lax
from jax.experimental import pallas as pl
from jax.experimental.pallas import tpu as pltpu

F32 = jnp.float32
BF16 = jnp.bfloat16
U32 = jnp.uint32
I32 = jnp.int32

D_MODEL = 1024
DEPTH = 4
A_WIDTH = 512
B_WIDTH = 1024
B_HEADS = 8
B_HEAD_DIM = 128
LRU_C = 8.0
M_HEADS = 4
M_QK_DIM = 128
M_V_DIM = 256
M_QK = 512
M_V = 1024
M_CHUNK = 128
X_HEADS = 4
X_HEAD_DIM = 256
N_EXPERTS = 32
TOP_K = 4
D_FF = 1024
SWIGLU_LIMIT = 7.0
SWIGLU_ALPHA = 1.702
DN_ALPHA = (2 * DEPTH) ** 0.25
LN_EPS = 1e-5
RMS_EPS = 1e-6
HALF = D_MODEL // 2

VMEM_LIMIT_BYTES = 56 * 1024 * 1024


def _cparams(n_grid):
    return pltpu.CompilerParams(
        dimension_semantics=("arbitrary",) * n_grid, vmem_limit_bytes=VMEM_LIMIT_BYTES)


def _layer_norm(y, g, b):
    mu = jnp.mean(y, axis=-1, keepdims=True)
    yc = y - mu
    var = jnp.mean(yc * yc, axis=-1, keepdims=True)
    return yc * lax.rsqrt(var + LN_EPS) * g + b


def _sigmoid(x):
    return 1.0 / (1.0 + jnp.exp(-x))


def _softplus(x):
    return jnp.maximum(x, 0.0) + jnp.log1p(jnp.exp(-jnp.abs(x)))


def _gelu_tanh(x):
    return 0.5 * x * (1.0 + jnp.tanh(0.7978845608028654 * (x + 0.044715 * (x * x * x))))


def _split_bf16(x):
    hi = x.astype(BF16)
    lo = (x - hi.astype(F32)).astype(BF16)
    return hi, lo


def _dot(a, b):
    return jnp.dot(a, b, preferred_element_type=F32)


def _dot_nt(a, b):
    return lax.dot_general(a, b, (((1,), (1,)), ((), ())), preferred_element_type=F32)


def _dot_tn(a, b):
    return lax.dot_general(a, b, (((0,), (0,)), ((), ())), preferred_element_type=F32)


def _pack_rows(y):
    bits = lax.bitcast_convert_type(y.astype(BF16).astype(F32), U32)
    return (bits[:, :HALF] >> 16) | (bits[:, HALF:] & jnp.uint32(0xFFFF0000))


def _unpack_rows(w):
    lo = lax.bitcast_convert_type(w << 16, F32)
    hi = lax.bitcast_convert_type(w & jnp.uint32(0xFFFF0000), F32)
    return jnp.concatenate([lo, hi], axis=1)


def _even_kernel(x_ref, win_ref, ca_ref, cb_ref, cbb_ref, wr_ref, br_ref, wi_ref, bi_ref, lam_ref,
                 wout_ref, g_ref, b_ref, o_ref,
                 av_scr, bu_scr, a_scr, b_scr, h_scr, carry_scr, *, ts):
    @pl.when(pl.program_id(1) == 0)
    def _():
        av_scr[0:8, :] = jnp.zeros((8, A_WIDTH), F32)
        bu_scr[0:8, :] = jnp.zeros((8, B_WIDTH), F32)
        carry_scr[...] = jnp.zeros((8, B_WIDTH), F32)

    x = x_ref[0]
    z = _dot(x.astype(BF16), win_ref[...])
    a_b = z[:, 0:A_WIDTH]
    a_c = z[:, A_WIDTH:2 * A_WIDTH]
    a_x = z[:, 2 * A_WIDTH:3 * A_WIDTH]
    b_u = z[:, 3 * A_WIDTH:3 * A_WIDTH + B_WIDTH]
    b_g = z[:, 3 * A_WIDTH + B_WIDTH:]

    av_scr[8:8 + ts, :] = a_c * a_x
    ca = ca_ref[...]
    conv_a = (ca[2:3, :] * av_scr[8:8 + ts, :] + ca[1:2, :] * av_scr[7:7 + ts, :]
              + ca[0:1, :] * av_scr[6:6 + ts, :])
    y_a = a_b * conv_a
    av_scr[0:8, :] = av_scr[ts:ts + 8, :]

    bu_scr[8:8 + ts, :] = b_u
    cb = cb_ref[...]
    u = (cb[3:4, :] * bu_scr[8:8 + ts, :] + cb[2:3, :] * bu_scr[7:7 + ts, :]
         + cb[1:2, :] * bu_scr[6:6 + ts, :] + cb[0:1, :] * bu_scr[5:5 + ts, :] + cbb_ref[...])
    bu_scr[0:8, :] = bu_scr[ts:ts + 8, :]

    ub = u.astype(BF16)
    r_parts, i_parts = [], []
    for h in range(B_HEADS):
        uh = ub[:, h * B_HEAD_DIM:(h + 1) * B_HEAD_DIM]
        r_parts.append(_dot(uh, wr_ref[h]))
        i_parts.append(_dot(uh, wi_ref[h]))
    r = _sigmoid(jnp.concatenate(r_parts, axis=1) + br_ref[...])
    ig = _sigmoid(jnp.concatenate(i_parts, axis=1) + bi_ref[...])
    log_a = (-LRU_C) * r * _softplus(-lam_ref[...])
    a = jnp.exp(log_a)
    a_scr[...] = a
    b_scr[...] = jnp.sqrt(1.0 - a * a) * (ig * u)

    row8 = lax.broadcasted_iota(I32, (8, B_WIDTH), 0)

    def group(i, carry):
        r0 = pl.multiple_of(i * 8, 8)
        ga = a_scr[pl.ds(r0, 8), :]
        gb = b_scr[pl.ds(r0, 8), :]
        for d in (1, 2, 4):
            keep = row8 >= d
            a_sh = jnp.where(keep, pltpu.roll(ga, d, 0), 1.0)
            b_sh = jnp.where(keep, pltpu.roll(gb, d, 0), 0.0)
            gb = ga * b_sh + gb
            ga = ga * a_sh
        hg = gb + ga * carry
        h_scr[pl.ds(r0, 8), :] = hg
        return jnp.broadcast_to(hg[7:8, :], (8, B_WIDTH))

    carry_scr[...] = lax.fori_loop(0, ts // 8, group, carry_scr[...], unroll=4)

    y_b = _gelu_tanh(b_g) * h_scr[...]
    mix = (_dot(y_a.astype(BF16), wout_ref[0:A_WIDTH, :])
           + _dot(y_b.astype(BF16), wout_ref[A_WIDTH:, :]))
    o_ref[0] = _layer_norm(DN_ALPHA * x + mix, g_ref[...], b_ref[...])


def _even_layer(x, w_in, conv_a, conv_b, conv_b_bias, w_r, b_r, w_i, b_i, lam, w_out, g, b, *, ts):
    bsz, s, _ = x.shape
    ev_in = 3 * A_WIDTH + 2 * B_WIDTH
    const2 = lambda bi, si: (0, 0)
    const3 = lambda bi, si: (0, 0, 0)
    return pl.pallas_call(
        functools.partial(_even_kernel, ts=ts),
        grid=(bsz, s // ts),
        in_specs=[
            pl.BlockSpec((1, ts, D_MODEL), lambda bi, si: (bi, si, 0)),
            pl.BlockSpec((D_MODEL, ev_in), const2),
            pl.BlockSpec((3, A_WIDTH), const2),
            pl.BlockSpec((4, B_WIDTH), const2),
            pl.BlockSpec((1, B_WIDTH), const2),
            pl.BlockSpec((B_HEADS, B_HEAD_DIM, B_HEAD_DIM), const3),
            pl.BlockSpec((1, B_WIDTH), const2),
            pl.BlockSpec((B_HEADS, B_HEAD_DIM, B_HEAD_DIM), const3),
            pl.BlockSpec((1, B_WIDTH), const2),
            pl.BlockSpec((1, B_WIDTH), const2),
            pl.BlockSpec((A_WIDTH + B_WIDTH, D_MODEL), const2),
            pl.BlockSpec((1, D_MODEL), const2),
            pl.BlockSpec((1, D_MODEL), const2),
        ],
        out_specs=pl.BlockSpec((1, ts, D_MODEL), lambda bi, si: (bi, si, 0)),
        out_shape=jax.ShapeDtypeStruct(x.shape, F32),
        scratch_shapes=[
            pltpu.VMEM((ts + 8, A_WIDTH), F32),
            pltpu.VMEM((ts + 8, B_WIDTH), F32),
            pltpu.VMEM((ts, B_WIDTH), F32),
            pltpu.VMEM((ts, B_WIDTH), F32),
            pltpu.VMEM((ts, B_WIDTH), F32),
            pltpu.VMEM((8, B_WIDTH), F32),
        ],
        compiler_params=_cparams(2),
        name="even_mixer",
    )(x, w_in.astype(BF16), conv_a, conv_b, conv_b_bias.reshape(1, -1), w_r.astype(BF16),
      b_r.reshape(1, -1), w_i.astype(BF16), b_i.reshape(1, -1), lam.reshape(1, -1),
      w_out.astype(BF16), g.reshape(1, -1), b.reshape(1, -1))


GATE_PAD = 128
V_EXT = M_V_DIM + 128


def _odd_kernel(x_ref, win_ref, wg_ref, wgt_ref, bgc_ref, bgr_ref, ng_ref, wout_ref, g_ref, b_ref,
                o_ref, c_scr, m_scr, h_scr, *, ts):
    L = M_CHUNK

    @pl.when(pl.program_id(1) == 0)
    def _():
        c_scr[...] = jnp.zeros(c_scr.shape, F32)
        m_scr[...] = jnp.zeros(m_scr.shape, F32)

    x = x_ref[0]
    x_hi, x_lo = _split_bf16(x)
    z = _dot(x_hi, win_ref[...])
    q_all = (z[:, 0:M_QK] * (M_QK_DIM ** -0.5)).astype(BF16)
    k_all = z[:, M_QK:2 * M_QK].astype(BF16)
    v_all = z[:, 2 * M_QK:2 * M_QK + M_V].astype(BF16)
    o_all = z[:, 2 * M_QK + M_V:]

    wg_hi, wg_lo = _split_bf16(wg_ref[...])
    gates_c = _dot(x_hi, wg_hi) + _dot(x_lo, wg_hi) + _dot(x_hi, wg_lo) + bgc_ref[...]
    wgt_hi, wgt_lo = _split_bf16(wgt_ref[...])
    gates_r = (_dot_nt(wgt_hi, x_hi) + _dot_nt(wgt_hi, x_lo) + _dot_nt(wgt_lo, x_hi)
               + bgr_ref[...])
    logf_c = -_softplus(-gates_c)
    logf_r = -_softplus(-gates_r)

    rows = lax.broadcasted_iota(I32, (L, L), 0)
    cols = lax.broadcasted_iota(I32, (L, L), 1)
    causal = rows >= cols
    tril = jnp.where(causal, 1.0, 0.0).astype(BF16)
    triu = jnp.where(rows <= cols, 1.0, 0.0).astype(BF16)
    lane0 = jnp.where(lax.broadcasted_iota(I32, (L, 128), 1) == 0, 1.0, 0.0).astype(BF16)

    for c in range(ts // L):
        sl = slice(c * L, (c + 1) * L)
        fc_hi, fc_lo = _split_bf16(logf_c[sl, :])
        fc_lo2 = (logf_c[sl, :] - fc_hi.astype(F32) - fc_lo.astype(F32)).astype(BF16)
        bcum_c = _dot(tril, fc_hi) + _dot(tril, fc_lo) + _dot(tril, fc_lo2)
        fr = logf_r[:, sl]
        fr_hi, fr_lo = _split_bf16(fr)
        fr_lo2 = (fr - fr_hi.astype(F32) - fr_lo.astype(F32)).astype(BF16)
        bcum_r = _dot(fr_hi, triu) + _dot(fr_lo, triu) + _dot(fr_lo2, triu)
        for h in range(M_HEADS):
            m_st = m_scr[h][0:1, 0:1]
            b_col = bcum_c[:, M_HEADS + h:M_HEADS + h + 1]
            i_col = gates_c[sl, h:h + 1]
            b_row = bcum_r[M_HEADS + h:M_HEADS + h + 1, :]
            i_row = gates_r[h:h + 1, sl]
            b_end = b_col[L - 1:L, :]

            d = jnp.where(causal, b_col + (i_row - b_row), -jnp.inf)
            inter = b_col + m_st
            m_t = jnp.maximum(inter, jnp.max(d, axis=1, keepdims=True))
            p = jnp.exp(d - m_t)
            w_inter = jnp.exp(inter - m_t)

            qh = q_all[sl, h * M_QK_DIM:(h + 1) * M_QK_DIM]
            kh = k_all[sl, h * M_QK_DIM:(h + 1) * M_QK_DIM]
            v_ext = jnp.concatenate([v_all[sl, h * M_V_DIM:(h + 1) * M_V_DIM], lane0], axis=1)
            qk = (_dot_nt(qh, kh) * p).astype(BF16)
            c_ext = c_scr[h]
            nd = _dot(qk, v_ext) + w_inter * _dot(qh, c_ext.astype(BF16))
            den = nd[:, M_V_DIM:M_V_DIM + 1]
            hh = nd[:, 0:M_V_DIM] / jnp.maximum(jnp.abs(den), jnp.exp(-m_t))

            g_col = b_end - b_col + i_col
            m_new = jnp.maximum(b_end + m_st, jnp.max(g_col, axis=0, keepdims=True))
            wg_col = jnp.exp(g_col - m_new)
            decay = jnp.exp(b_end + m_st - m_new)
            kv = _dot_tn(kh, (wg_col * v_ext.astype(F32)).astype(BF16))
            c_scr[h] = decay * c_ext + kv
            m_scr[h] = jnp.broadcast_to(m_new, (8, 128))

            hn = hh * lax.rsqrt(jnp.mean(hh * hh, axis=-1, keepdims=True) + RMS_EPS)
            h_scr[sl, h * M_V_DIM:(h + 1) * M_V_DIM] = hn

    gated = _sigmoid(o_all) * (h_scr[...] * ng_ref[...])
    mix = _dot(gated.astype(BF16), wout_ref[...])
    o_ref[0] = _layer_norm(DN_ALPHA * x + mix, g_ref[...], b_ref[...])


def _odd_layer(x, w_in, b_gates, norm_g, w_out, g, b, *, ts):
    bsz, s, _ = x.shape
    n_main = 2 * M_QK + 2 * M_V
    w_main = w_in[:, :n_main].astype(BF16)
    w_gate = w_in[:, n_main:]
    wg_pad = jnp.pad(w_gate, ((0, 0), (0, GATE_PAD - 2 * M_HEADS)))
    bg_col = jnp.pad(b_gates, (0, GATE_PAD - 2 * M_HEADS)).reshape(1, GATE_PAD)
    bg_row = b_gates.reshape(2 * M_HEADS, 1)
    const2 = lambda bi, si: (0, 0)
    return pl.pallas_call(
        functools.partial(_odd_kernel, ts=ts),
        grid=(bsz, s // ts),
        in_specs=[
            pl.BlockSpec((1, ts, D_MODEL), lambda bi, si: (bi, si, 0)),
            pl.BlockSpec((D_MODEL, n_main), const2),
            pl.BlockSpec((D_MODEL, GATE_PAD), const2),
            pl.BlockSpec((2 * M_HEADS, D_MODEL), const2),
            pl.BlockSpec((1, GATE_PAD), const2),
            pl.BlockSpec((2 * M_HEADS, 1), const2),
            pl.BlockSpec((1, M_V), const2),
            pl.BlockSpec((M_V, D_MODEL), const2),
            pl.BlockSpec((1, D_MODEL), const2),
            pl.BlockSpec((1, D_MODEL), const2),
        ],
        out_specs=pl.BlockSpec((1, ts, D_MODEL), lambda bi, si: (bi, si, 0)),
        out_shape=jax.ShapeDtypeStruct(x.shape, F32),
        scratch_shapes=[
            pltpu.VMEM((M_HEADS, M_QK_DIM, V_EXT), F32),
            pltpu.VMEM((M_HEADS, 8, 128), F32),
            pltpu.VMEM((ts, M_V), F32),
        ],
        compiler_params=_cparams(2),
        name="odd_mixer",
    )(x, w_main, wg_pad, w_gate.T, bg_col, bg_row, norm_g.reshape(1, -1), w_out.astype(BF16),
      g.reshape(1, -1), b.reshape(1, -1))


def _kv_kernel(mem_ref, wk_ref, wv_ref, k_ref, v_ref):
    m = mem_ref[0].astype(BF16)
    k_ref[0] = _dot(m, wk_ref[...]).astype(BF16)
    v_ref[0] = _dot(m, wv_ref[...]).astype(BF16)


def _kv_proj(mem, wk, wv):
    bsz, n_mem, _ = mem.shape
    const2 = lambda bi: (0, 0)
    blk = pl.BlockSpec((1, n_mem, D_MODEL), lambda bi: (bi, 0, 0))
    return pl.pallas_call(
        _kv_kernel,
        grid=(bsz,),
        in_specs=[blk, pl.BlockSpec((D_MODEL, D_MODEL), const2),
                  pl.BlockSpec((D_MODEL, D_MODEL), const2)],
        out_specs=[blk, blk],
        out_shape=[jax.ShapeDtypeStruct(mem.shape, BF16)] * 2,
        compiler_params=_cparams(1),
        name="memory_kv",
    )(mem, wk.astype(BF16), wv.astype(BF16))


def _attn_kernel(x_ref, k_ref, v_ref, wq_ref, wo_ref, g_ref, b_ref, wrt_ref, brt_ref,
                 x2_ref, x2p_ref, idx_ref, tw_ref, rank_ref, cnt_ref, cnt_scr, *, ts):
    @pl.when((pl.program_id(0) == 0) & (pl.program_id(1) == 0))
    def _():
        cnt_scr[...] = jnp.zeros(cnt_scr.shape, F32)

    x = x_ref[0]
    q = _dot(x.astype(BF16), wq_ref[...]).astype(BF16)
    heads = []
    for h in range(X_HEADS):
        hs = slice(h * X_HEAD_DIM, (h + 1) * X_HEAD_DIM)
        sc = _dot_nt(q[:, hs], k_ref[0][:, hs]) * (X_HEAD_DIM ** -0.5)
        e = jnp.exp(sc - jnp.max(sc, axis=-1, keepdims=True))
        p = e / jnp.sum(e, axis=-1, keepdims=True)
        heads.append(_dot(p.astype(BF16), v_ref[0][:, hs]))
    att = _dot(jnp.concatenate(heads, axis=1).astype(BF16), wo_ref[...])
    x2 = _layer_norm(DN_ALPHA * x + att, g_ref[...], b_ref[...])
    x2_ref[0] = x2
    x2p_ref[0] = _pack_rows(x2)

    x_hi, x_lo = _split_bf16(x2)
    w_hi, w_lo = _split_bf16(wrt_ref[...])
    logits = _dot_nt(w_hi, x_hi) + _dot_nt(w_hi, x_lo) + _dot_nt(w_lo, x_hi) + brt_ref[...]
    e_iota = lax.broadcasted_iota(I32, (N_EXPERTS, ts), 0)
    onehots, vals, ids = [], [], []
    for _ in range(TOP_K):
        mx = jnp.max(logits, axis=0, keepdims=True)
        sel = jnp.min(jnp.where(logits == mx, e_iota, N_EXPERTS), axis=0, keepdims=True)
        oh = e_iota == sel
        onehots.append(oh)
        vals.append(mx)
        ids.append(sel)
        logits = jnp.where(oh, -jnp.inf, logits)
    exps = [jnp.exp(v - vals[0]) for v in vals]
    tot = exps[0] + exps[1] + exps[2] + exps[3]
    idx_ref[0] = jnp.concatenate(ids, axis=0)
    tw_ref[0] = jnp.concatenate([ex / tot for ex in exps], axis=0)

    member = jnp.where(onehots[0] | onehots[1] | onehots[2] | onehots[3], 1.0, 0.0)
    t_r = lax.broadcasted_iota(I32, (ts, ts), 0)
    t_c = lax.broadcasted_iota(I32, (ts, ts), 1)
    before = jnp.where(t_r < t_c, 1.0, 0.0).astype(BF16)
    base = cnt_scr[...][:, 0:1] + _dot(member.astype(BF16), before)
    ranks = [jnp.sum(jnp.where(oh, base, 0.0), axis=0, keepdims=True) for oh in onehots]
    rank_ref[0] = jnp.concatenate(ranks, axis=0).astype(I32)
    cnt_scr[...] = cnt_scr[...] + jnp.sum(member, axis=1, keepdims=True)
    cnt_ref[...] = cnt_scr[...].astype(I32)


def _attn_router(x, k_mem, v_mem, wq, wo, g, b, w_router, b_router, *, ts):
    bsz, s, _ = x.shape
    n_mem = k_mem.shape[1]
    const2 = lambda bi, si: (0, 0)
    tok = lambda bi, si: (bi, si, 0)
    lanes = lambda bi, si: (bi, 0, si)
    return pl.pallas_call(
        functools.partial(_attn_kernel, ts=ts),
        grid=(bsz, s // ts),
        in_specs=[
            pl.BlockSpec((1, ts, D_MODEL), tok),
            pl.BlockSpec((1, n_mem, D_MODEL), lambda bi, si: (bi, 0, 0)),
            pl.BlockSpec((1, n_mem, D_MODEL), lambda bi, si: (bi, 0, 0)),
            pl.BlockSpec((D_MODEL, D_MODEL), const2),
            pl.BlockSpec((D_MODEL, D_MODEL), const2),
            pl.BlockSpec((1, D_MODEL), const2),
            pl.BlockSpec((1, D_MODEL), const2),
            pl.BlockSpec((N_EXPERTS, D_MODEL), const2),
            pl.BlockSpec((N_EXPERTS, 1), const2),
        ],
        out_specs=[
            pl.BlockSpec((1, ts, D_MODEL), tok),
            pl.BlockSpec((1, ts, HALF), tok),
            pl.BlockSpec((1, TOP_K, ts), lanes),
            pl.BlockSpec((1, TOP_K, ts), lanes),
            pl.BlockSpec((1, TOP_K, ts), lanes),
            pl.BlockSpec((N_EXPERTS, 128), const2),
        ],
        out_shape=[
            jax.ShapeDtypeStruct((bsz, s, D_MODEL), F32),
            jax.ShapeDtypeStruct((bsz, s, HALF), U32),
            jax.ShapeDtypeStruct((bsz, TOP_K, s), I32),
            jax.ShapeDtypeStruct((bsz, TOP_K, s), F32),
            jax.ShapeDtypeStruct((bsz, TOP_K, s), I32),
            jax.ShapeDtypeStruct((N_EXPERTS, 128), I32),
        ],
        scratch_shapes=[pltpu.VMEM((N_EXPERTS, 128), F32)],
        compiler_params=_cparams(2),
        name="memory_attention_router",
    )(x, k_mem, v_mem, wq.astype(BF16), wo.astype(BF16), g.reshape(1, -1), b.reshape(1, -1),
      w_router.T, b_router.reshape(-1, 1))


def _dispatch_kernel(slot_ref, x_hbm, xs_hbm, sem, *, tt):
    base = pl.program_id(0) * tt

    def body(i, carry):
        src = x_hbm.at[pl.ds(base + i, 1), :]
        for k in range(TOP_K):
            dst = xs_hbm.at[pl.ds(slot_ref[0, 0, k * tt + i], 1), :]
            pltpu.make_async_copy(src, dst, sem).start()
        return carry

    lax.fori_loop(0, tt, body, 0, unroll=8)
    pltpu.make_async_copy(xs_hbm.at[pl.ds(0, TOP_K * tt), :],
                          xs_hbm.at[pl.ds(0, TOP_K * tt), :], sem).wait()


def _dispatch(x2p, slot_tiles, n_rows, *, tt):
    n_tok = x2p.shape[0]
    return pl.pallas_call(
        functools.partial(_dispatch_kernel, tt=tt),
        grid=(n_tok // tt,),
        in_specs=[
            pl.BlockSpec((1, 1, TOP_K * tt), lambda i: (i, 0, 0), memory_space=pltpu.SMEM),
            pl.BlockSpec(memory_space=pl.ANY),
        ],
        out_specs=pl.BlockSpec(memory_space=pl.ANY),
        out_shape=jax.ShapeDtypeStruct((n_rows, HALF), U32),
        scratch_shapes=[pltpu.SemaphoreType.DMA],
        compiler_params=pltpu.CompilerParams(
            dimension_semantics=("arbitrary",), has_side_effects=True),
        name="moe_dispatch",
    )(slot_tiles, x2p)


def _ffn_kernel(te_ref, nused_ref, xs_ref, w1_ref, b1_ref, w2_ref, b2_ref, ys_ref):
    del te_ref

    @pl.when(pl.program_id(0) < nused_ref[0])
    def _():
        xt = _unpack_rows(xs_ref[...]).astype(BF16)
        gate = _dot(xt, w1_ref[0, :, 0:D_FF]) + b1_ref[0, :, 0:D_FF]
        up = _dot(xt, w1_ref[0, :, D_FF:]) + b1_ref[0, :, D_FF:]
        gate = jnp.minimum(gate, SWIGLU_LIMIT)
        up = jnp.clip(up, -SWIGLU_LIMIT, SWIGLU_LIMIT)
        hid = (up + 1.0) * gate * _sigmoid(SWIGLU_ALPHA * gate)
        ys_ref[...] = _pack_rows(_dot(hid.astype(BF16), w2_ref[0]) + b2_ref[0])


def _grouped_ffn(xs, tile_expert, n_used, w1, b1, w2, b2, *, tm):
    n_rows = xs.shape[0]
    n_tiles = n_rows // tm
    row_blk = lambda i, te, nu: (jnp.minimum(i, nu[0] - 1), 0)
    exp_blk = lambda i, te, nu: (te[i], 0, 0)
    grid_spec = pltpu.PrefetchScalarGridSpec(
        num_scalar_prefetch=2,
        grid=(n_tiles,),
        in_specs=[
            pl.BlockSpec((tm, HALF), row_blk),
            pl.BlockSpec((1, D_MODEL, 2 * D_FF), exp_blk),
            pl.BlockSpec((1, 1, 2 * D_FF), exp_blk),
            pl.BlockSpec((1, D_FF, D_MODEL), exp_blk),
            pl.BlockSpec((1, 1, D_MODEL), exp_blk),
        ],
        out_specs=pl.BlockSpec((tm, HALF), row_blk),
    )
    return pl.pallas_call(
        _ffn_kernel,
        grid_spec=grid_spec,
        out_shape=jax.ShapeDtypeStruct((n_rows, HALF), U32),
        compiler_params=_cparams(1),
        name="moe_grouped_ffn",
    )(tile_expert, n_used, xs, w1, b1.reshape(N_EXPERTS, 1, -1), w2, b2.reshape(N_EXPERTS, 1, -1))


def _combine_kernel(slot_ref, x_ref, tw_ref, g_ref, b_ref, ys_hbm, o_ref, rows_scr, sem, *, tt):
    def body(i, carry):
        for k in range(TOP_K):
            src = ys_hbm.at[pl.ds(slot_ref[0, 0, k * tt + i], 1), :]
            pltpu.make_async_copy(src, rows_scr.at[pl.ds(k * tt + i, 1), :], sem).start()
        return carry

    lax.fori_loop(0, tt, body, 0, unroll=8)
    pltpu.make_async_copy(ys_hbm.at[pl.ds(0, TOP_K * tt), :], rows_scr, sem).wait()

    tw = tw_ref[...]
    y = tw[:, 0:1] * _unpack_rows(rows_scr[0:tt, :])
    for k in range(1, TOP_K):
        y = y + tw[:, k:k + 1] * _unpack_rows(rows_scr[k * tt:(k + 1) * tt, :])
    o_ref[...] = _layer_norm(DN_ALPHA * x_ref[...] + y, g_ref[...], b_ref[...])


def _combine(x2, ys, slot_tiles, tw_tok, g, b, *, tt):
    n_tok = x2.shape[0]
    const2 = lambda i: (0, 0)
    return pl.pallas_call(
        functools.partial(_combine_kernel, tt=tt),
        grid=(n_tok // tt,),
        in_specs=[
            pl.BlockSpec((1, 1, TOP_K * tt), lambda i: (i, 0, 0), memory_space=pltpu.SMEM),
            pl.BlockSpec((tt, D_MODEL), lambda i: (i, 0)),
            pl.BlockSpec((tt, TOP_K), lambda i: (i, 0)),
            pl.BlockSpec((1, D_MODEL), const2),
            pl.BlockSpec((1, D_MODEL), const2),
            pl.BlockSpec(memory_space=pl.ANY),
        ],
        out_specs=pl.BlockSpec((tt, D_MODEL), lambda i: (i, 0)),
        out_shape=jax.ShapeDtypeStruct((n_tok, D_MODEL), F32),
        scratch_shapes=[pltpu.VMEM((TOP_K * tt, HALF), U32), pltpu.SemaphoreType.DMA],
        compiler_params=_cparams(1),
        name="moe_combine",
    )(slot_tiles, x2, tw_tok, g.reshape(1, -1), b.reshape(1, -1), ys)


def _slot_tiles(slot, tt):
    bsz, _, s = slot.shape
    t = slot.reshape(bsz, TOP_K, s // tt, tt).transpose(0, 2, 1, 3)
    return t.reshape(bsz * (s // tt), 1, TOP_K * tt)


def _moe_layer(x2, x2p, idx, tw, rank, counts, w1, b1, w2, b2, g, b, *, tm, tt_dispatch, tt_combine):
    bsz, s, _ = x2.shape
    n_tok = bsz * s
    n_tiles = (n_tok * TOP_K) // tm + N_EXPERTS
    n_rows = n_tiles * tm

    tiles_e = (counts + tm - 1) // tm
    tile_end = jnp.cumsum(tiles_e)
    start_e = (tile_end - tiles_e) * tm
    slot = jnp.take(start_e, idx) + rank
    n_used = tile_end[-1]
    tile_ids = jnp.minimum(jnp.arange(n_tiles, dtype=I32), n_used - 1)
    tile_expert = jnp.sum(tile_ids[:, None] >= tile_end[None, :], axis=1).astype(I32)

    xs = _dispatch(x2p.reshape(n_tok, HALF), _slot_tiles(slot, tt_dispatch), n_rows, tt=tt_dispatch)
    ys = _grouped_ffn(xs, tile_expert, n_used.reshape(1).astype(I32), w1, b1, w2, b2, tm=tm)
    tw_tok = tw.transpose(0, 2, 1).reshape(n_tok, TOP_K)
    out = _combine(x2.reshape(n_tok, D_MODEL), ys, _slot_tiles(slot, tt_combine), tw_tok, g, b,
                   tt=tt_combine)
    return out.reshape(bsz, s, D_MODEL)


def _pick_tile(n, want):
    t = min(n, want)
    while n % t:
        t //= 2
    return t


def kernel(x, mem, ev_w_in, ev_conv_a, ev_conv_b, ev_conv_b_bias, ev_w_rgate, ev_b_rgate, ev_w_igate, ev_b_igate, ev_lambda, ev_w_out, od_w_in, od_b_gates, od_norm_g, od_w_out, xa_wq, xa_wk, xa_wv, xa_wo, moe_w_router, moe_b_router, moe_w1, moe_b1, moe_w2, moe_b2, ln_g, ln_b):
    bsz, s, _ = x.shape
    ts_even = _pick_tile(s, 512)
    ts_odd = _pick_tile(s, 256)
    ts_attn = _pick_tile(s, 512)
    tm = 512
    tt_dispatch = _pick_tile(s, 1024)
    tt_combine = _pick_tile(s, 256)

    w1_bf = moe_w1.astype(BF16)
    w2_bf = moe_w2.astype(BF16)
    for layer in range(DEPTH):
        j = layer // 2
        if layer % 2 == 0:
            x = _even_layer(x, ev_w_in[j], ev_conv_a[j], ev_conv_b[j], ev_conv_b_bias[j],
                            ev_w_rgate[j], ev_b_rgate[j], ev_w_igate[j], ev_b_igate[j],
                            ev_lambda[j], ev_w_out[j], ln_g[layer, 0], ln_b[layer, 0], ts=ts_even)
        else:
            x = _odd_layer(x, od_w_in[j], od_b_gates[j], od_norm_g[j], od_w_out[j],
                           ln_g[layer, 0], ln_b[layer, 0], ts=ts_odd)
        k_mem, v_mem = _kv_proj(mem, xa_wk[layer], xa_wv[layer])
        x2, x2p, idx, tw, rank, cnt = _attn_router(
            x, k_mem, v_mem, xa_wq[layer], xa_wo[layer], ln_g[layer, 1], ln_b[layer, 1],
            moe_w_router[layer], moe_b_router[layer], ts=ts_attn)
        x = _moe_layer(x2, x2p, idx, tw, rank, cnt[:, 0], w1_bf[layer], moe_b1[layer],
                       w2_bf[layer], moe_b2[layer], ln_g[layer, 2], ln_b[layer, 2],
                       tm=tm, tt_dispatch=tt_dispatch, tt_combine=tt_combine)
    return x
```

```python
import functools

import jax
import jax.numpy as jnp
from jax import lax
from jax.experimental import pallas as pl
from jax.experimental.pallas import tpu as pltpu

F32 = jnp.float32
BF16 = jnp.bfloat16
U32 = jnp.uint32
I32 = jnp.int32

D_MODEL = 1024
DEPTH = 4
A_WIDTH = 512
B_WIDTH = 1024
B_HEADS = 8
B_HEAD_DIM = 128
LRU_C = 8.0
M_HEADS = 4
M_QK_DIM = 128
M_V_DIM = 256
M_QK = 512
M_V = 1024
M_CHUNK = 128
X_HEADS = 4
X_HEAD_DIM = 256
N_EXPERTS = 32
TOP_K = 4
D_FF = 1024
SWIGLU_LIMIT = 7.0
SWIGLU_ALPHA = 1.702
DN_ALPHA = (2 * DEPTH) ** 0.25
LN_EPS = 1e-5
RMS_EPS = 1e-6
HALF = D_MODEL // 2

VMEM_LIMIT_BYTES = 56 * 1024 * 1024


def _cparams(n_grid):
    return pltpu.CompilerParams(
        dimension_semantics=("arbitrary",) * n_grid, vmem_limit_bytes=VMEM_LIMIT_BYTES)


def _layer_norm(y, g, b):
    mu = jnp.mean(y, axis=-1, keepdims=True)
    yc = y - mu
    var = jnp.mean(yc * yc, axis=-1, keepdims=True)
    return yc * lax.rsqrt(var + LN_EPS) * g + b


def _sigmoid(x):
    return 1.0 / (1.0 + jnp.exp(-x))


def _softplus(x):
    return jnp.maximum(x, 0.0) + jnp.log1p(jnp.exp(-jnp.abs(x)))


def _gelu_tanh(x):
    return 0.5 * x * (1.0 + jnp.tanh(0.7978845608028654 * (x + 0.044715 * (x * x * x))))


def _split_bf16(x):
    hi = x.astype(BF16)
    lo = (x - hi.astype(F32)).astype(BF16)
    return hi, lo


def _dot(a, b):
    return jnp.dot(a, b, preferred_element_type=F32)


def _dot_nt(a, b):
    return lax.dot_general(a, b, (((1,), (1,)), ((), ())), preferred_element_type=F32)


def _dot_tn(a, b):
    return lax.dot_general(a, b, (((0,), (0,)), ((), ())), preferred_element_type=F32)


def _pack_rows(y):
    bits = lax.bitcast_convert_type(y.astype(BF16).astype(F32), U32)
    return (bits[:, :HALF] >> 16) | (bits[:, HALF:] & jnp.uint32(0xFFFF0000))


def _unpack_rows(w):
    lo = lax.bitcast_convert_type(w << 16, F32)
    hi = lax.bitcast_convert_type(w & jnp.uint32(0xFFFF0000), F32)
    return jnp.concatenate([lo, hi], axis=1)


def _even_kernel(x_ref, win_ref, ca_ref, cb_ref, cbb_ref, wr_ref, br_ref, wi_ref, bi_ref, lam_ref,
                 wout_ref, g_ref, b_ref, o_ref,
                 av_scr, bu_scr, a_scr, b_scr, h_scr, carry_scr, *, ts):
    @pl.when(pl.program_id(1) == 0)
    def _():
        av_scr[0:8, :] = jnp.zeros((8, A_WIDTH), F32)
        bu_scr[0:8, :] = jnp.zeros((8, B_WIDTH), F32)
        carry_scr[...] = jnp.zeros((8, B_WIDTH), F32)

    x = x_ref[0]
    z = _dot(x.astype(BF16), win_ref[...])
    a_b = z[:, 0:A_WIDTH]
    a_c = z[:, A_WIDTH:2 * A_WIDTH]
    a_x = z[:, 2 * A_WIDTH:3 * A_WIDTH]
    b_u = z[:, 3 * A_WIDTH:3 * A_WIDTH + B_WIDTH]
    b_g = z[:, 3 * A_WIDTH + B_WIDTH:]

    av_scr[8:8 + ts, :] = a_c * a_x
    ca = ca_ref[...]
    conv_a = (ca[2:3, :] * av_scr[8:8 + ts, :] + ca[1:2, :] * av_scr[7:7 + ts, :]
              + ca[0:1, :] * av_scr[6:6 + ts, :])
    y_a = a_b * conv_a
    av_scr[0:8, :] = av_scr[ts:ts + 8, :]

    bu_scr[8:8 + ts, :] = b_u
    cb = cb_ref[...]
    u = (cb[3:4, :] * bu_scr[8:8 + ts, :] + cb[2:3, :] * bu_scr[7:7 + ts, :]
         + cb[1:2, :] * bu_scr[6:6 + ts, :] + cb[0:1, :] * bu_scr[5:5 + ts, :] + cbb_ref[...])
    bu_scr[0:8, :] = bu_scr[ts:ts + 8, :]

    ub = u.astype(BF16)
    r_parts, i_parts = [], []
    for h in range(B_HEADS):
        uh = ub[:, h * B_HEAD_DIM:(h + 1) * B_HEAD_DIM]
        r_parts.append(_dot(uh, wr_ref[h]))
        i_parts.append(_dot(uh, wi_ref[h]))
    r = _sigmoid(jnp.concatenate(r_parts, axis=1) + br_ref[...])
    ig = _sigmoid(jnp.concatenate(i_parts, axis=1) + bi_ref[...])
    log_a = (-LRU_C) * r * _softplus(-lam_ref[...])
    a = jnp.exp(log_a)
    a_scr[...] = a
    b_scr[...] = jnp.sqrt(1.0 - a * a) * (ig * u)

    row8 = lax.broadcasted_iota(I32, (8, B_WIDTH), 0)

    def group(i, carry):
        r0 = pl.multiple_of(i * 8, 8)
        ga = a_scr[pl.ds(r0, 8), :]
        gb = b_scr[pl.ds(r0, 8), :]
        for d in (1, 2, 4):
            keep = row8 >= d
            a_sh = jnp.where(keep, pltpu.roll(ga, d, 0), 1.0)
            b_sh = jnp.where(keep, pltpu.roll(gb, d, 0), 0.0)
            gb = ga * b_sh + gb
            ga = ga * a_sh
        hg = gb + ga * carry
        h_scr[pl.ds(r0, 8), :] = hg
        return jnp.broadcast_to(hg[7:8, :], (8, B_WIDTH))

    carry_scr[...] = lax.fori_loop(0, ts // 8, group, carry_scr[...], unroll=4)

    y_b = _gelu_tanh(b_g) * h_scr[...]
    mix = (_dot(y_a.astype(BF16), wout_ref[0:A_WIDTH, :])
           + _dot(y_b.astype(BF16), wout_ref[A_WIDTH:, :]))
    o_ref[0] = _layer_norm(DN_ALPHA * x + mix, g_ref[...], b_ref[...])


def _even_layer(x, w_in, conv_a, conv_b, conv_b_bias, w_r, b_r, w_i, b_i, lam, w_out, g, b, *, ts):
    bsz, s, _ = x.shape
    ev_in = 3 * A_WIDTH + 2 * B_WIDTH
    const2 = lambda bi, si: (0, 0)
    const3 = lambda bi, si: (0, 0, 0)
    return pl.pallas_call(
        functools.partial(_even_kernel, ts=ts),
        grid=(bsz, s // ts),
        in_specs=[
            pl.BlockSpec((1, ts, D_MODEL), lambda bi, si: (bi, si, 0)),
            pl.BlockSpec((D_MODEL, ev_in), const2),
            pl.BlockSpec((3, A_WIDTH), const2),
            pl.BlockSpec((4, B_WIDTH), const2),
            pl.BlockSpec((1, B_WIDTH), const2),
            pl.BlockSpec((B_HEADS, B_HEAD_DIM, B_HEAD_DIM), const3),
            pl.BlockSpec((1, B_WIDTH), const2),
            pl.BlockSpec((B_HEADS, B_HEAD_DIM, B_HEAD_DIM), const3),
            pl.BlockSpec((1, B_WIDTH), const2),
            pl.BlockSpec((1, B_WIDTH), const2),
            pl.BlockSpec((A_WIDTH + B_WIDTH, D_MODEL), const2),
            pl.BlockSpec((1, D_MODEL), const2),
            pl.BlockSpec((1, D_MODEL), const2),
        ],
        out_specs=pl.BlockSpec((1, ts, D_MODEL), lambda bi, si: (bi, si, 0)),
        out_shape=jax.ShapeDtypeStruct(x.shape, F32),
        scratch_shapes=[
            pltpu.VMEM((ts + 8, A_WIDTH), F32),
            pltpu.VMEM((ts + 8, B_WIDTH), F32),
            pltpu.VMEM((ts, B_WIDTH), F32),
            pltpu.VMEM((ts, B_WIDTH), F32),
            pltpu.VMEM((ts, B_WIDTH), F32),
            pltpu.VMEM((8, B_WIDTH), F32),
        ],
        compiler_params=_cparams(2),
        name="even_mixer",
    )(x, w_in.astype(BF16), conv_a, conv_b, conv_b_bias.reshape(1, -1), w_r.astype(BF16),
      b_r.reshape(1, -1), w_i.astype(BF16), b_i.reshape(1, -1), lam.reshape(1, -1),
      w_out.astype(BF16), g.reshape(1, -1), b.reshape(1, -1))


GATE_PAD = 128
V_EXT = M_V_DIM + 128


def _odd_kernel(x_ref, win_ref, wg_ref, wgt_ref, bgc_ref, bgr_ref, ng_ref, wout_ref, g_ref, b_ref,
                o_ref, c_scr, m_scr, h_scr, *, ts):
    L = M_CHUNK

    @pl.when(pl.program_id(1) == 0)
    def _():
        c_scr[...] = jnp.zeros(c_scr.shape, F32)
        m_scr[...] = jnp.zeros(m_scr.shape, F32)

    x = x_ref[0]
    x_hi, x_lo = _split_bf16(x)
    z = _dot(x_hi, win_ref[...])
    q_all = (z[:, 0:M_QK] * (M_QK_DIM ** -0.5)).astype(BF16)
    k_all = z[:, M_QK:2 * M_QK].astype(BF16)
    v_all = z[:, 2 * M_QK:2 * M_QK + M_V].astype(BF16)
    o_all = z[:, 2 * M_QK + M_V:]

    wg_hi, wg_lo = _split_bf16(wg_ref[...])
    gates_c = _dot(x_hi, wg_hi) + _dot(x_lo, wg_hi) + _dot(x_hi, wg_lo) + bgc_ref[...]
    wgt_hi, wgt_lo = _split_bf16(wgt_ref[...])
    gates_r = (_dot_nt(wgt_hi, x_hi) + _dot_nt(wgt_hi, x_lo) + _dot_nt(wgt_lo, x_hi)
               + bgr_ref[...])
    logf_c = -_softplus(-gates_c)
    logf_r = -_softplus(-gates_r)

    rows = lax.broadcasted_iota(I32, (L, L), 0)
    cols = lax.broadcasted_iota(I32, (L, L), 1)
    causal = rows >= cols
    tril = jnp.where(causal, 1.0, 0.0).astype(BF16)
    triu = jnp.where(rows <= cols, 1.0, 0.0).astype(BF16)
    lane0 = jnp.where(lax.broadcasted_iota(I32, (L, 128), 1) == 0, 1.0, 0.0).astype(BF16)

    for c in range(ts // L):
        sl = slice(c * L, (c + 1) * L)
        fc_hi, fc_lo = _split_bf16(logf_c[sl, :])
        fc_lo2 = (logf_c[sl, :] - fc_hi.astype(F32) - fc_lo.astype(F32)).astype(BF16)
        bcum_c = _dot(tril, fc_hi) + _dot(tril, fc_lo) + _dot(tril, fc_lo2)
        fr = logf_r[:, sl]
        fr_hi, fr_lo = _split_bf16(fr)
        fr_lo2 = (fr - fr_hi.astype(F32) - fr_lo.astype(F32)).astype(BF16)
        bcum_r = _dot(fr_hi, triu) + _dot(fr_lo, triu) + _dot(fr_lo2, triu)
        for h in range(M_HEADS):
            m_st = m_scr[h][0:1, 0:1]
            b_col = bcum_c[:, M_HEADS + h:M_HEADS + h + 1]
            i_col = gates_c[sl, h:h + 1]
            b_row = bcum_r[M_HEADS + h:M_HEADS + h + 1, :]
            i_row = gates_r[h:h + 1, sl]
            b_end = b_col[L - 1:L, :]

            d = jnp.where(causal, b_col + (i_row - b_row), -jnp.inf)
            inter = b_col + m_st
            m_t = jnp.maximum(inter, jnp.max(d, axis=1, keepdims=True))
            p = jnp.exp(d - m_t)
            w_inter = jnp.exp(inter - m_t)

            qh = q_all[sl, h * M_QK_DIM:(h + 1) * M_QK_DIM]
            kh = k_all[sl, h * M_QK_DIM:(h + 1) * M_QK_DIM]
            v_ext = jnp.concatenate([v_all[sl, h * M_V_DIM:(h + 1) * M_V_DIM], lane0], axis=1)
            qk = (_dot_nt(qh, kh) * p).astype(BF16)
            c_ext = c_scr[h]
            nd = _dot(qk, v_ext) + w_inter * _dot(qh, c_ext.astype(BF16))
            den = nd[:, M_V_DIM:M_V_DIM + 1]
            hh = nd[:, 0:M_V_DIM] / jnp.maximum(jnp.abs(den), jnp.exp(-m_t))

            g_col = b_end - b_col + i_col
            m_new = jnp.maximum(b_end + m_st, jnp.max(g_col, axis=0, keepdims=True))
            wg_col = jnp.exp(g_col - m_new)
            decay = jnp.exp(b_end + m_st - m_new)
            kv = _dot_tn(kh, (wg_col * v_ext.astype(F32)).astype(BF16))
            c_scr[h] = decay * c_ext + kv
            m_scr[h] = jnp.broadcast_to(m_new, (8, 128))

            hn = hh * lax.rsqrt(jnp.mean(hh * hh, axis=-1, keepdims=True) + RMS_EPS)
            h_scr[sl, h * M_V_DIM:(h + 1) * M_V_DIM] = hn

    gated = _sigmoid(o_all) * (h_scr[...] * ng_ref[...])
    mix = _dot(gated.astype(BF16), wout_ref[...])
    o_ref[0] = _layer_norm(DN_ALPHA * x + mix, g_ref[...], b_ref[...])


def _odd_layer(x, w_in, b_gates, norm_g, w_out, g, b, *, ts):
    bsz, s, _ = x.shape
    n_main = 2 * M_QK + 2 * M_V
    w_main = w_in[:, :n_main].astype(BF16)
    w_gate = w_in[:, n_main:]
    wg_pad = jnp.pad(w_gate, ((0, 0), (0, GATE_PAD - 2 * M_HEADS)))
    bg_col = jnp.pad(b_gates, (0, GATE_PAD - 2 * M_HEADS)).reshape(1, GATE_PAD)
    bg_row = b_gates.reshape(2 * M_HEADS, 1)
    const2 = lambda bi, si: (0, 0)
    return pl.pallas_call(
        functools.partial(_odd_kernel, ts=ts),
        grid=(bsz, s // ts),
        in_specs=[
            pl.BlockSpec((1, ts, D_MODEL), lambda bi, si: (bi, si, 0)),
            pl.BlockSpec((D_MODEL, n_main), const2),
            pl.BlockSpec((D_MODEL, GATE_PAD), const2),
            pl.BlockSpec((2 * M_HEADS, D_MODEL), const2),
            pl.BlockSpec((1, GATE_PAD), const2),
            pl.BlockSpec((2 * M_HEADS, 1), const2),
            pl.BlockSpec((1, M_V), const2),
            pl.BlockSpec((M_V, D_MODEL), const2),
            pl.BlockSpec((1, D_MODEL), const2),
            pl.BlockSpec((1, D_MODEL), const2),
        ],
        out_specs=pl.BlockSpec((1, ts, D_MODEL), lambda bi, si: (bi, si, 0)),
        out_shape=jax.ShapeDtypeStruct(x.shape, F32),
        scratch_shapes=[
            pltpu.VMEM((M_HEADS, M_QK_DIM, V_EXT), F32),
            pltpu.VMEM((M_HEADS, 8, 128), F32),
            pltpu.VMEM((ts, M_V), F32),
        ],
        compiler_params=_cparams(2),
        name="odd_mixer",
    )(x, w_main, wg_pad, w_gate.T, bg_col, bg_row, norm_g.reshape(1, -1), w_out.astype(BF16),
      g.reshape(1, -1), b.reshape(1, -1))


def _kv_kernel(mem_ref, wk_ref, wv_ref, k_ref, v_ref):
    m = mem_ref[0].astype(BF16)
    k_ref[0] = _dot(m, wk_ref[...]).astype(BF16)
    v_ref[0] = _dot(m, wv_ref[...]).astype(BF16)


def _kv_proj(mem, wk, wv):
    bsz, n_mem, _ = mem.shape
    const2 = lambda bi: (0, 0)
    blk = pl.BlockSpec((1, n_mem, D_MODEL), lambda bi: (bi, 0, 0))
    return pl.pallas_call(
        _kv_kernel,
        grid=(bsz,),
        in_specs=[blk, pl.BlockSpec((D_MODEL, D_MODEL), const2),
                  pl.BlockSpec((D_MODEL, D_MODEL), const2)],
        out_specs=[blk, blk],
        out_shape=[jax.ShapeDtypeStruct(mem.shape, BF16)] * 2,
        compiler_params=_cparams(1),
        name="memory_kv",
    )(mem, wk.astype(BF16), wv.astype(BF16))


def _attn_kernel(x_ref, k_ref, v_ref, wq_ref, wo_ref, g_ref, b_ref, wrt_ref, brt_ref,
                 x2_ref, x2p_ref, idx_ref, tw_ref, rank_ref, cnt_ref, cnt_scr, *, ts):
    @pl.when((pl.program_id(0) == 0) & (pl.program_id(1) == 0))
    def _():
        cnt_scr[...] = jnp.zeros(cnt_scr.shape, F32)

    x = x_ref[0]
    q = _dot(x.astype(BF16), wq_ref[...]).astype(BF16)
    heads = []
    for h in range(X_HEADS):
        hs = slice(h * X_HEAD_DIM, (h + 1) * X_HEAD_DIM)
        sc = _dot_nt(q[:, hs], k_ref[0][:, hs]) * (X_HEAD_DIM ** -0.5)
        e = jnp.exp(sc - jnp.max(sc, axis=-1, keepdims=True))
        p = e / jnp.sum(e, axis=-1, keepdims=True)
        heads.append(_dot(p.astype(BF16), v_ref[0][:, hs]))
    att = _dot(jnp.concatenate(heads, axis=1).astype(BF16), wo_ref[...])
    x2 = _layer_norm(DN_ALPHA * x + att, g_ref[...], b_ref[...])
    x2_ref[0] = x2
    x2p_ref[0] = _pack_rows(x2)

    x_hi, x_lo = _split_bf16(x2)
    w_hi, w_lo = _split_bf16(wrt_ref[...])
    logits = _dot_nt(w_hi, x_hi) + _dot_nt(w_hi, x_lo) + _dot_nt(w_lo, x_hi) + brt_ref[...]
    e_iota = lax.broadcasted_iota(I32, (N_EXPERTS, ts), 0)
    onehots, vals, ids = [], [], []
    for _ in range(TOP_K):
        mx = jnp.max(logits, axis=0, keepdims=True)
        sel = jnp.min(jnp.where(logits == mx, e_iota, N_EXPERTS), axis=0, keepdims=True)
        oh = e_iota == sel
        onehots.append(oh)
        vals.append(mx)
        ids.append(sel)
        logits = jnp.where(oh, -jnp.inf, logits)
    exps = [jnp.exp(v - vals[0]) for v in vals]
    tot = exps[0] + exps[1] + exps[2] + exps[3]
    idx_ref[0] = jnp.concatenate(ids, axis=0)
    tw_ref[0] = jnp.concatenate([ex / tot for ex in exps], axis=0)

    member = jnp.where(onehots[0] | onehots[1] | onehots[2] | onehots[3], 1.0, 0.0)
    t_r = lax.broadcasted_iota(I32, (ts, ts), 0)
    t_c = lax.broadcasted_iota(I32, (ts, ts), 1)
    before = jnp.where(t_r < t_c, 1.0, 0.0).astype(BF16)
    base = cnt_scr[...][:, 0:1] + _dot(member.astype(BF16), before)
    ranks = [jnp.sum(jnp.where(oh, base, 0.0), axis=0, keepdims=True) for oh in onehots]
    rank_ref[0] = jnp.concatenate(ranks, axis=0).astype(I32)
    cnt_scr[...] = cnt_scr[...] + jnp.sum(member, axis=1, keepdims=True)
    cnt_ref[...] = cnt_scr[...].astype(I32)


def _attn_router(x, k_mem, v_mem, wq, wo, g, b, w_router, b_router, *, ts):
    bsz, s, _ = x.shape
    n_mem = k_mem.shape[1]
    const2 = lambda bi, si: (0, 0)
    tok = lambda bi, si: (bi, si, 0)
    lanes = lambda bi, si: (bi, 0, si)
    return pl.pallas_call(
        functools.partial(_attn_kernel, ts=ts),
        grid=(bsz, s // ts),
        in_specs=[
            pl.BlockSpec((1, ts, D_MODEL), tok),
            pl.BlockSpec((1, n_mem, D_MODEL), lambda bi, si: (bi, 0, 0)),
            pl.BlockSpec((1, n_mem, D_MODEL), lambda bi, si: (bi, 0, 0)),
            pl.BlockSpec((D_MODEL, D_MODEL), const2),
            pl.BlockSpec((D_MODEL, D_MODEL), const2),
            pl.BlockSpec((1, D_MODEL), const2),
            pl.BlockSpec((1, D_MODEL), const2),
            pl.BlockSpec((N_EXPERTS, D_MODEL), const2),
            pl.BlockSpec((N_EXPERTS, 1), const2),
        ],
        out_specs=[
            pl.BlockSpec((1, ts, D_MODEL), tok),
            pl.BlockSpec((1, ts, HALF), tok),
            pl.BlockSpec((1, TOP_K, ts), lanes),
            pl.BlockSpec((1, TOP_K, ts), lanes),
            pl.BlockSpec((1, TOP_K, ts), lanes),
            pl.BlockSpec((N_EXPERTS, 128), const2),
        ],
        out_shape=[
            jax.ShapeDtypeStruct((bsz, s, D_MODEL), F32),
            jax.ShapeDtypeStruct((bsz, s, HALF), U32),
            jax.ShapeDtypeStruct((bsz, TOP_K, s), I32),
            jax.ShapeDtypeStruct((bsz, TOP_K, s), F32),
            jax.ShapeDtypeStruct((bsz, TOP_K, s), I32),
            jax.ShapeDtypeStruct((N_EXPERTS, 128), I32),
        ],
        scratch_shapes=[pltpu.VMEM((N_EXPERTS, 128), F32)],
        compiler_params=_cparams(2),
        name="memory_attention_router",
    )(x, k_mem, v_mem, wq.astype(BF16), wo.astype(BF16), g.reshape(1, -1), b.reshape(1, -1),
      w_router.T, b_router.reshape(-1, 1))


def _dispatch_kernel(last_ref, nused_ref, slot_ref, x_ref, xs_hbm, zero_scr, sem, zsem, *, tt, tm,
                     n_tiles):
    @pl.when(pl.program_id(0) == 0)
    def _():
        zero_scr[...] = jnp.zeros(zero_scr.shape, U32)

        def zero_copy(row):
            return pltpu.make_async_copy(
                zero_scr, xs_hbm.at[pl.ds(pl.multiple_of(row, tm), tm), :], zsem)

        for e in range(N_EXPERTS):
            @pl.when(last_ref[e] >= 0)
            def _():
                zero_copy(last_ref[e]).start()

        def start_tail(i, carry):
            zero_copy(i * tm).start()
            return carry

        lax.fori_loop(nused_ref[0], n_tiles, start_tail, 0)

        for e in range(N_EXPERTS):
            @pl.when(last_ref[e] >= 0)
            def _():
                zero_copy(last_ref[e]).wait()

        def wait_tail(i, carry):
            zero_copy(i * tm).wait()
            return carry

        lax.fori_loop(nused_ref[0], n_tiles, wait_tail, 0)

    def body(i, carry):
        src = x_ref.at[pl.ds(i, 1), :]
        for k in range(TOP_K):
            dst = xs_hbm.at[pl.ds(slot_ref[0, 0, k * tt + i], 1), :]
            pltpu.make_async_copy(src, dst, sem).start()
        return carry

    lax.fori_loop(0, tt, body, 0, unroll=8)
    pltpu.make_async_copy(xs_hbm.at[pl.ds(0, TOP_K * tt), :],
                          xs_hbm.at[pl.ds(0, TOP_K * tt), :], sem).wait()


def _dispatch(x2p, slot_tiles, last_tile_row, n_used, n_rows, *, tt, tm):
    n_tok = x2p.shape[0]
    grid_spec = pltpu.PrefetchScalarGridSpec(
        num_scalar_prefetch=2,
        grid=(n_tok // tt,),
        in_specs=[
            pl.BlockSpec((1, 1, TOP_K * tt), lambda i, la, nu: (i, 0, 0),
                         memory_space=pltpu.SMEM),
            pl.BlockSpec((tt, HALF), lambda i, la, nu: (i, 0)),
        ],
        out_specs=pl.BlockSpec(memory_space=pl.ANY),
        scratch_shapes=[pltpu.VMEM((tm, HALF), U32), pltpu.SemaphoreType.DMA,
                        pltpu.SemaphoreType.DMA],
    )
    return pl.pallas_call(
        functools.partial(_dispatch_kernel, tt=tt, tm=tm, n_tiles=n_rows // tm),
        grid_spec=grid_spec,
        out_shape=jax.ShapeDtypeStruct((n_rows, HALF), U32),
        compiler_params=pltpu.CompilerParams(
            dimension_semantics=("arbitrary",), has_side_effects=True),
        name="moe_dispatch",
    )(last_tile_row, n_used, slot_tiles, x2p)


def _ffn_kernel(te_ref, nused_ref, xs_ref, w1_ref, b1_ref, w2_ref, b2_ref, ys_ref):
    del te_ref

    @pl.when(pl.program_id(0) < nused_ref[0])
    def _():
        xt = _unpack_rows(xs_ref[...]).astype(BF16)
        gate = _dot(xt, w1_ref[0, :, 0:D_FF]) + b1_ref[0, :, 0:D_FF]
        up = _dot(xt, w1_ref[0, :, D_FF:]) + b1_ref[0, :, D_FF:]
        gate = jnp.minimum(gate, SWIGLU_LIMIT)
        up = jnp.clip(up, -SWIGLU_LIMIT, SWIGLU_LIMIT)
        hid = (up + 1.0) * gate * _sigmoid(SWIGLU_ALPHA * gate)
        ys_ref[...] = _pack_rows(_dot(hid.astype(BF16), w2_ref[0]) + b2_ref[0])

    @pl.when(pl.program_id(0) >= nused_ref[0])
    def _():
        ys_ref[...] = jnp.zeros(ys_ref.shape, U32)


def _grouped_ffn(xs, tile_expert, n_used, w1, b1, w2, b2, *, tm):
    n_rows = xs.shape[0]
    n_tiles = n_rows // tm
    row_blk = lambda i, te, nu: (jnp.minimum(i, nu[0] - 1), 0)
    exp_blk = lambda i, te, nu: (te[i], 0, 0)
    grid_spec = pltpu.PrefetchScalarGridSpec(
        num_scalar_prefetch=2,
        grid=(n_tiles,),
        in_specs=[
            pl.BlockSpec((tm, HALF), row_blk),
            pl.BlockSpec((1, D_MODEL, 2 * D_FF), exp_blk),
            pl.BlockSpec((1, 1, 2 * D_FF), exp_blk),
            pl.BlockSpec((1, D_FF, D_MODEL), exp_blk),
            pl.BlockSpec((1, 1, D_MODEL), exp_blk),
        ],
        out_specs=pl.BlockSpec((tm, HALF), lambda i, te, nu: (i, 0)),
    )
    return pl.pallas_call(
        _ffn_kernel,
        grid_spec=grid_spec,
        out_shape=jax.ShapeDtypeStruct((n_rows, HALF), U32),
        compiler_params=_cparams(1),
        name="moe_grouped_ffn",
    )(tile_expert, n_used, xs, w1, b1.reshape(N_EXPERTS, 1, -1), w2, b2.reshape(N_EXPERTS, 1, -1))


def _combine_kernel(slot_ref, x_ref, tw_ref, g_ref, b_ref, ys_hbm, o_ref, rows_scr, sem, *, tt):
    def body(i, carry):
        for k in range(TOP_K):
            src = ys_hbm.at[pl.ds(slot_ref[0, 0, k * tt + i], 1), :]
            pltpu.make_async_copy(src, rows_scr.at[pl.ds(k * tt + i, 1), :], sem).start()
        return carry

    lax.fori_loop(0, tt, body, 0, unroll=8)
    pltpu.make_async_copy(ys_hbm.at[pl.ds(0, TOP_K * tt), :], rows_scr, sem).wait()

    tw = tw_ref[...]
    y = tw[:, 0:1] * _unpack_rows(rows_scr[0:tt, :])
    for k in range(1, TOP_K):
        y = y + tw[:, k:k + 1] * _unpack_rows(rows_scr[k * tt:(k + 1) * tt, :])
    o_ref[...] = _layer_norm(DN_ALPHA * x_ref[...] + y, g_ref[...], b_ref[...])


def _combine(x2, ys, slot_tiles, tw_tok, g, b, *, tt):
    n_tok = x2.shape[0]
    const2 = lambda i: (0, 0)
    return pl.pallas_call(
        functools.partial(_combine_kernel, tt=tt),
        grid=(n_tok // tt,),
        in_specs=[
            pl.BlockSpec((1, 1, TOP_K * tt), lambda i: (i, 0, 0), memory_space=pltpu.SMEM),
            pl.BlockSpec((tt, D_MODEL), lambda i: (i, 0)),
            pl.BlockSpec((tt, TOP_K), lambda i: (i, 0)),
            pl.BlockSpec((1, D_MODEL), const2),
            pl.BlockSpec((1, D_MODEL), const2),
            pl.BlockSpec(memory_space=pl.ANY),
        ],
        out_specs=pl.BlockSpec((tt, D_MODEL), lambda i: (i, 0)),
        out_shape=jax.ShapeDtypeStruct((n_tok, D_MODEL), F32),
        scratch_shapes=[pltpu.VMEM((TOP_K * tt, HALF), U32), pltpu.SemaphoreType.DMA],
        compiler_params=_cparams(1),
        name="moe_combine",
    )(slot_tiles, x2, tw_tok, g.reshape(1, -1), b.reshape(1, -1), ys)


def _slot_tiles(slot, tt):
    bsz, _, s = slot.shape
    t = slot.reshape(bsz, TOP_K, s // tt, tt).transpose(0, 2, 1, 3)
    return t.reshape(bsz * (s // tt), 1, TOP_K * tt)


def _moe_layer(x2, x2p, idx, tw, rank, counts, w1, b1, w2, b2, g, b, *, tm, tt_dispatch, tt_combine):
    bsz, s, _ = x2.shape
    n_tok = bsz * s
    n_tiles = (n_tok * TOP_K) // tm + N_EXPERTS
    n_rows = n_tiles * tm

    tiles_e = (counts + tm - 1) // tm
    tile_end = jnp.cumsum(tiles_e)
    start_e = (tile_end - tiles_e) * tm
    experts = jnp.arange(N_EXPERTS, dtype=I32).reshape(N_EXPERTS, 1, 1, 1)
    slot = rank + jnp.sum(jnp.where(idx[None] == experts, start_e.reshape(-1, 1, 1, 1), 0), axis=0)
    n_used = tile_end[-1]
    last_tile_row = jnp.where(tiles_e > 0, (tile_end - 1) * tm, -1).astype(I32)
    n_used_arr = n_used.reshape(1).astype(I32)
    tile_ids = jnp.minimum(jnp.arange(n_tiles, dtype=I32), n_used - 1)
    tile_expert = jnp.sum(tile_ids[:, None] >= tile_end[None, :], axis=1).astype(I32)

    xs = _dispatch(x2p.reshape(n_tok, HALF), _slot_tiles(slot, tt_dispatch), last_tile_row,
                   n_used_arr, n_rows, tt=tt_dispatch, tm=tm)
    ys = _grouped_ffn(xs, tile_expert, n_used_arr, w1, b1, w2, b2, tm=tm)
    tw_tok = tw.transpose(0, 2, 1).reshape(n_tok, TOP_K)
    out = _combine(x2.reshape(n_tok, D_MODEL), ys, _slot_tiles(slot, tt_combine), tw_tok, g, b,
                   tt=tt_combine)
    return out.reshape(bsz, s, D_MODEL)


def _pick_tile(n, want):
    t = min(n, want)
    while n % t:
        t //= 2
    return t


def kernel(x, mem, ev_w_in, ev_conv_a, ev_conv_b, ev_conv_b_bias, ev_w_rgate, ev_b_rgate, ev_w_igate, ev_b_igate, ev_lambda, ev_w_out, od_w_in, od_b_gates, od_norm_g, od_w_out, xa_wq, xa_wk, xa_wv, xa_wo, moe_w_router, moe_b_router, moe_w1, moe_b1, moe_w2, moe_b2, ln_g, ln_b):
    bsz, s, _ = x.shape
    ts_even = _pick_tile(s, 512)
    ts_odd = _pick_tile(s, 256)
    ts_attn = _pick_tile(s, 512)
    tm = 512
    tt_dispatch = _pick_tile(s, 1024)
    tt_combine = _pick_tile(s, 256)

    w1_bf = moe_w1.astype(BF16)
    w2_bf = moe_w2.astype(BF16)
    for layer in range(DEPTH):
        j = layer // 2
        if layer % 2 == 0:
            x = _even_layer(x, ev_w_in[j], ev_conv_a[j], ev_conv_b[j], ev_conv_b_bias[j],
                            ev_w_rgate[j], ev_b_rgate[j], ev_w_igate[j], ev_b_igate[j],
                            ev_lambda[j], ev_w_out[j], ln_g[layer, 0], ln_b[layer, 0], ts=ts_even)
        else:
            x = _odd_layer(x, od_w_in[j], od_b_gates[j], od_norm_g[j], od_w_out[j],
                           ln_g[layer, 0], ln_b[layer, 0], ts=ts_odd)
        k_mem, v_mem = _kv_proj(mem, xa_wk[layer], xa_wv[layer])
        x2, x2p, idx, tw, rank, cnt = _attn_router(
            x, k_mem, v_mem, xa_wq[layer], xa_wo[layer], ln_g[layer, 1], ln_b[layer, 1],
            moe_w_router[layer], moe_b_router[layer], ts=ts_attn)
        x = _moe_layer(x2, x2p, idx, tw, rank, cnt[:, 0], w1_bf[layer], moe_b1[layer],
                       w2_bf[layer], moe_b2[layer], ln_g[layer, 2], ln_b[layer, 2],
                       tm=tm, tt_dispatch=tt_dispatch, tt_combine=tt_combine)
    return x
```

```python
import functools

import jax
import jax.numpy as jnp
from jax import lax
from jax.experimental import pallas as pl
from jax.experimental.pallas import tpu as pltpu

F32 = jnp.float32
BF16 = jnp.bfloat16
U32 = jnp.uint32
I32 = jnp.int32

D_MODEL = 1024
DEPTH = 4
A_WIDTH = 512
B_WIDTH = 1024
B_HEADS = 8
B_HEAD_DIM = 128
LRU_C = 8.0
M_HEADS = 4
M_QK_DIM = 128
M_V_DIM = 256
M_QK = 512
M_V = 1024
M_CHUNK = 128
X_HEADS = 4
X_HEAD_DIM = 256
N_EXPERTS = 32
TOP_K = 4
D_FF = 1024
SWIGLU_LIMIT = 7.0
SWIGLU_ALPHA = 1.702
DN_ALPHA = (2 * DEPTH) ** 0.25
LN_EPS = 1e-5
RMS_EPS = 1e-6
HALF = D_MODEL // 2
SEG_ALIGN = 8
MXU_DIM = 256

VMEM_LIMIT_BYTES = 56 * 1024 * 1024


def _cparams(n_grid):
    return pltpu.CompilerParams(
        dimension_semantics=("arbitrary",) * n_grid, vmem_limit_bytes=VMEM_LIMIT_BYTES)


def _layer_norm(y, g, b):
    mu = jnp.mean(y, axis=-1, keepdims=True)
    yc = y - mu
    var = jnp.mean(yc * yc, axis=-1, keepdims=True)
    return yc * lax.rsqrt(var + LN_EPS) * g + b


def _sigmoid(x):
    return 1.0 / (1.0 + jnp.exp(-x))


def _softplus(x):
    return jnp.maximum(x, 0.0) + jnp.log1p(jnp.exp(-jnp.abs(x)))


def _gelu_tanh(x):
    return 0.5 * x * (1.0 + jnp.tanh(0.7978845608028654 * (x + 0.044715 * (x * x * x))))


def _split_bf16(x):
    hi = x.astype(BF16)
    lo = (x - hi.astype(F32)).astype(BF16)
    return hi, lo


def _dot(a, b):
    return jnp.dot(a, b, preferred_element_type=F32)


def _dot_nt(a, b):
    return lax.dot_general(a, b, (((1,), (1,)), ((), ())), preferred_element_type=F32)


def _dot_tn(a, b):
    return lax.dot_general(a, b, (((0,), (0,)), ((), ())), preferred_element_type=F32)


def _pack_rows(y):
    bits = lax.bitcast_convert_type(y.astype(BF16).astype(F32), U32)
    return (bits[:, :HALF] >> 16) | (bits[:, HALF:] & jnp.uint32(0xFFFF0000))


def _unpack_rows(w):
    lo = lax.bitcast_convert_type(w << 16, F32)
    hi = lax.bitcast_convert_type(w & jnp.uint32(0xFFFF0000), F32)
    return jnp.concatenate([lo, hi], axis=1)


def _even_kernel(x_ref, win_ref, ca_ref, cb_ref, cbb_ref, wr_ref, br_ref, wi_ref, bi_ref, lam_ref,
                 wout_ref, g_ref, b_ref, o_ref,
                 av_scr, bu_scr, a_scr, b_scr, h_scr, carry_scr, *, ts):
    @pl.when(pl.program_id(1) == 0)
    def _():
        av_scr[0:8, :] = jnp.zeros((8, A_WIDTH), F32)
        bu_scr[0:8, :] = jnp.zeros((8, B_WIDTH), F32)
        carry_scr[...] = jnp.zeros((8, B_WIDTH), F32)

    x = x_ref[0]
    z = _dot(x.astype(BF16), win_ref[...])
    a_b = z[:, 0:A_WIDTH]
    a_c = z[:, A_WIDTH:2 * A_WIDTH]
    a_x = z[:, 2 * A_WIDTH:3 * A_WIDTH]
    b_u = z[:, 3 * A_WIDTH:3 * A_WIDTH + B_WIDTH]
    b_g = z[:, 3 * A_WIDTH + B_WIDTH:]

    av_scr[8:8 + ts, :] = a_c * a_x
    ca = ca_ref[...]
    conv_a = (ca[2:3, :] * av_scr[8:8 + ts, :] + ca[1:2, :] * av_scr[7:7 + ts, :]
              + ca[0:1, :] * av_scr[6:6 + ts, :])
    y_a = a_b * conv_a
    av_scr[0:8, :] = av_scr[ts:ts + 8, :]

    bu_scr[8:8 + ts, :] = b_u
    cb = cb_ref[...]
    u = (cb[3:4, :] * bu_scr[8:8 + ts, :] + cb[2:3, :] * bu_scr[7:7 + ts, :]
         + cb[1:2, :] * bu_scr[6:6 + ts, :] + cb[0:1, :] * bu_scr[5:5 + ts, :] + cbb_ref[...])
    bu_scr[0:8, :] = bu_scr[ts:ts + 8, :]

    ub = u.astype(BF16)
    r_parts, i_parts = [], []
    for h in range(B_HEADS):
        uh = ub[:, h * B_HEAD_DIM:(h + 1) * B_HEAD_DIM]
        r_parts.append(_dot(uh, wr_ref[h]))
        i_parts.append(_dot(uh, wi_ref[h]))
    r = _sigmoid(jnp.concatenate(r_parts, axis=1) + br_ref[...])
    ig = _sigmoid(jnp.concatenate(i_parts, axis=1) + bi_ref[...])
    log_a = (-LRU_C) * r * _softplus(-lam_ref[...])
    a = jnp.exp(log_a)
    a_scr[...] = a
    b_scr[...] = jnp.sqrt(1.0 - a * a) * (ig * u)

    row8 = lax.broadcasted_iota(I32, (8, B_WIDTH), 0)

    def group(i, carry):
        r0 = pl.multiple_of(i * 8, 8)
        ga = a_scr[pl.ds(r0, 8), :]
        gb = b_scr[pl.ds(r0, 8), :]
        for d in (1, 2, 4):
            keep = row8 >= d
            a_sh = jnp.where(keep, pltpu.roll(ga, d, 0), 1.0)
            b_sh = jnp.where(keep, pltpu.roll(gb, d, 0), 0.0)
            gb = ga * b_sh + gb
            ga = ga * a_sh
        hg = gb + ga * carry
        h_scr[pl.ds(r0, 8), :] = hg
        return jnp.broadcast_to(hg[7:8, :], (8, B_WIDTH))

    carry_scr[...] = lax.fori_loop(0, ts // 8, group, carry_scr[...], unroll=4)

    y_b = _gelu_tanh(b_g) * h_scr[...]
    mix = (_dot(y_a.astype(BF16), wout_ref[0:A_WIDTH, :])
           + _dot(y_b.astype(BF16), wout_ref[A_WIDTH:, :]))
    o_ref[0] = _layer_norm(DN_ALPHA * x + mix, g_ref[...], b_ref[...])


def _even_layer(x, w_in, conv_a, conv_b, conv_b_bias, w_r, b_r, w_i, b_i, lam, w_out, g, b, *, ts):
    bsz, s, _ = x.shape
    ev_in = 3 * A_WIDTH + 2 * B_WIDTH
    const2 = lambda bi, si: (0, 0)
    const3 = lambda bi, si: (0, 0, 0)
    return pl.pallas_call(
        functools.partial(_even_kernel, ts=ts),
        grid=(bsz, s // ts),
        in_specs=[
            pl.BlockSpec((1, ts, D_MODEL), lambda bi, si: (bi, si, 0)),
            pl.BlockSpec((D_MODEL, ev_in), const2),
            pl.BlockSpec((3, A_WIDTH), const2),
            pl.BlockSpec((4, B_WIDTH), const2),
            pl.BlockSpec((1, B_WIDTH), const2),
            pl.BlockSpec((B_HEADS, B_HEAD_DIM, B_HEAD_DIM), const3),
            pl.BlockSpec((1, B_WIDTH), const2),
            pl.BlockSpec((B_HEADS, B_HEAD_DIM, B_HEAD_DIM), const3),
            pl.BlockSpec((1, B_WIDTH), const2),
            pl.BlockSpec((1, B_WIDTH), const2),
            pl.BlockSpec((A_WIDTH + B_WIDTH, D_MODEL), const2),
            pl.BlockSpec((1, D_MODEL), const2),
            pl.BlockSpec((1, D_MODEL), const2),
        ],
        out_specs=pl.BlockSpec((1, ts, D_MODEL), lambda bi, si: (bi, si, 0)),
        out_shape=jax.ShapeDtypeStruct(x.shape, F32),
        scratch_shapes=[
            pltpu.VMEM((ts + 8, A_WIDTH), F32),
            pltpu.VMEM((ts + 8, B_WIDTH), F32),
            pltpu.VMEM((ts, B_WIDTH), F32),
            pltpu.VMEM((ts, B_WIDTH), F32),
            pltpu.VMEM((ts, B_WIDTH), F32),
            pltpu.VMEM((8, B_WIDTH), F32),
        ],
        compiler_params=_cparams(2),
        name="even_mixer",
    )(x, w_in.astype(BF16), conv_a, conv_b, conv_b_bias.reshape(1, -1), w_r.astype(BF16),
      b_r.reshape(1, -1), w_i.astype(BF16), b_i.reshape(1, -1), lam.reshape(1, -1),
      w_out.astype(BF16), g.reshape(1, -1), b.reshape(1, -1))


GATE_PAD = 128
V_EXT = M_V_DIM + 128


def _odd_kernel(x_ref, win_ref, wg_ref, wgt_ref, bgc_ref, bgr_ref, ng_ref, wout_ref, g_ref, b_ref,
                o_ref, c_scr, m_scr, h_scr, *, ts):
    L = M_CHUNK

    @pl.when(pl.program_id(1) == 0)
    def _():
        c_scr[...] = jnp.zeros(c_scr.shape, F32)
        m_scr[...] = jnp.zeros(m_scr.shape, F32)

    x = x_ref[0]
    x_hi, x_lo = _split_bf16(x)
    z = _dot(x_hi, win_ref[...])
    q_all = (z[:, 0:M_QK] * (M_QK_DIM ** -0.5)).astype(BF16)
    k_all = z[:, M_QK:2 * M_QK].astype(BF16)
    v_all = z[:, 2 * M_QK:2 * M_QK + M_V].astype(BF16)
    o_all = z[:, 2 * M_QK + M_V:]

    wg_hi, wg_lo = _split_bf16(wg_ref[...])
    gates_c = _dot(x_hi, wg_hi) + _dot(x_lo, wg_hi) + _dot(x_hi, wg_lo) + bgc_ref[...]
    wgt_hi, wgt_lo = _split_bf16(wgt_ref[...])
    gates_r = (_dot_nt(wgt_hi, x_hi) + _dot_nt(wgt_hi, x_lo) + _dot_nt(wgt_lo, x_hi)
               + bgr_ref[...])
    logf_c = -_softplus(-gates_c)
    logf_r = -_softplus(-gates_r)

    rows = lax.broadcasted_iota(I32, (L, L), 0)
    cols = lax.broadcasted_iota(I32, (L, L), 1)
    causal = rows >= cols
    tril = jnp.where(causal, 1.0, 0.0).astype(BF16)
    triu = jnp.where(rows <= cols, 1.0, 0.0).astype(BF16)
    lane0 = jnp.where(lax.broadcasted_iota(I32, (L, 128), 1) == 0, 1.0, 0.0).astype(BF16)

    for c in range(ts // L):
        sl = slice(c * L, (c + 1) * L)
        fc_hi, fc_lo = _split_bf16(logf_c[sl, :])
        fc_lo2 = (logf_c[sl, :] - fc_hi.astype(F32) - fc_lo.astype(F32)).astype(BF16)
        bcum_c = _dot(tril, fc_hi) + _dot(tril, fc_lo) + _dot(tril, fc_lo2)
        fr = logf_r[:, sl]
        fr_hi, fr_lo = _split_bf16(fr)
        fr_lo2 = (fr - fr_hi.astype(F32) - fr_lo.astype(F32)).astype(BF16)
        bcum_r = _dot(fr_hi, triu) + _dot(fr_lo, triu) + _dot(fr_lo2, triu)
        for h in range(M_HEADS):
            m_st = m_scr[h][0:1, 0:1]
            b_col = bcum_c[:, M_HEADS + h:M_HEADS + h + 1]
            i_col = gates_c[sl, h:h + 1]
            b_row = bcum_r[M_HEADS + h:M_HEADS + h + 1, :]
            i_row = gates_r[h:h + 1, sl]
            b_end = b_col[L - 1:L, :]

            d = jnp.where(causal, b_col + (i_row - b_row), -jnp.inf)
            inter = b_col + m_st
            m_t = jnp.maximum(inter, jnp.max(d, axis=1, keepdims=True))
            p = jnp.exp(d - m_t)
            w_inter = jnp.exp(inter - m_t)

            qh = q_all[sl, h * M_QK_DIM:(h + 1) * M_QK_DIM]
            kh = k_all[sl, h * M_QK_DIM:(h + 1) * M_QK_DIM]
            v_ext = jnp.concatenate([v_all[sl, h * M_V_DIM:(h + 1) * M_V_DIM], lane0], axis=1)
            qk = (_dot_nt(qh, kh) * p).astype(BF16)
            c_ext = c_scr[h]
            nd = _dot(qk, v_ext) + w_inter * _dot(qh, c_ext.astype(BF16))
            den = nd[:, M_V_DIM:M_V_DIM + 1]
            hh = nd[:, 0:M_V_DIM] / jnp.maximum(jnp.abs(den), jnp.exp(-m_t))

            g_col = b_end - b_col + i_col
            m_new = jnp.maximum(b_end + m_st, jnp.max(g_col, axis=0, keepdims=True))
            wg_col = jnp.exp(g_col - m_new)
            decay = jnp.exp(b_end + m_st - m_new)
            kv = _dot_tn(kh, (wg_col * v_ext.astype(F32)).astype(BF16))
            c_scr[h] = decay * c_ext + kv
            m_scr[h] = jnp.broadcast_to(m_new, (8, 128))

            hn = hh * lax.rsqrt(jnp.mean(hh * hh, axis=-1, keepdims=True) + RMS_EPS)
            h_scr[sl, h * M_V_DIM:(h + 1) * M_V_DIM] = hn

    gated = _sigmoid(o_all) * (h_scr[...] * ng_ref[...])
    mix = _dot(gated.astype(BF16), wout_ref[...])
    o_ref[0] = _layer_norm(DN_ALPHA * x + mix, g_ref[...], b_ref[...])


def _odd_layer(x, w_in, b_gates, norm_g, w_out, g, b, *, ts):
    bsz, s, _ = x.shape
    n_main = 2 * M_QK + 2 * M_V
    w_main = w_in[:, :n_main].astype(BF16)
    w_gate = w_in[:, n_main:]
    wg_pad = jnp.pad(w_gate, ((0, 0), (0, GATE_PAD - 2 * M_HEADS)))
    bg_col = jnp.pad(b_gates, (0, GATE_PAD - 2 * M_HEADS)).reshape(1, GATE_PAD)
    bg_row = b_gates.reshape(2 * M_HEADS, 1)
    const2 = lambda bi, si: (0, 0)
    return pl.pallas_call(
        functools.partial(_odd_kernel, ts=ts),
        grid=(bsz, s // ts),
        in_specs=[
            pl.BlockSpec((1, ts, D_MODEL), lambda bi, si: (bi, si, 0)),
            pl.BlockSpec((D_MODEL, n_main), const2),
            pl.BlockSpec((D_MODEL, GATE_PAD), const2),
            pl.BlockSpec((2 * M_HEADS, D_MODEL), const2),
            pl.BlockSpec((1, GATE_PAD), const2),
            pl.BlockSpec((2 * M_HEADS, 1), const2),
            pl.BlockSpec((1, M_V), const2),
            pl.BlockSpec((M_V, D_MODEL), const2),
            pl.BlockSpec((1, D_MODEL), const2),
            pl.BlockSpec((1, D_MODEL), const2),
        ],
        out_specs=pl.BlockSpec((1, ts, D_MODEL), lambda bi, si: (bi, si, 0)),
        out_shape=jax.ShapeDtypeStruct(x.shape, F32),
        scratch_shapes=[
            pltpu.VMEM((M_HEADS, M_QK_DIM, V_EXT), F32),
            pltpu.VMEM((M_HEADS, 8, 128), F32),
            pltpu.VMEM((ts, M_V), F32),
        ],
        compiler_params=_cparams(2),
        name="odd_mixer",
    )(x, w_main, wg_pad, w_gate.T, bg_col, bg_row, norm_g.reshape(1, -1), w_out.astype(BF16),
      g.reshape(1, -1), b.reshape(1, -1))


def _kv_kernel(mem_ref, wk_ref, wv_ref, k_ref, v_ref):
    m = mem_ref[0].astype(BF16)
    k_ref[0] = _dot(m, wk_ref[...]).astype(BF16)
    v_ref[0] = _dot(m, wv_ref[...]).astype(BF16)


def _kv_proj(mem, wk, wv):
    bsz, n_mem, _ = mem.shape
    const2 = lambda bi: (0, 0)
    blk = pl.BlockSpec((1, n_mem, D_MODEL), lambda bi: (bi, 0, 0))
    return pl.pallas_call(
        _kv_kernel,
        grid=(bsz,),
        in_specs=[blk, pl.BlockSpec((D_MODEL, D_MODEL), const2),
                  pl.BlockSpec((D_MODEL, D_MODEL), const2)],
        out_specs=[blk, blk],
        out_shape=[jax.ShapeDtypeStruct(mem.shape, BF16)] * 2,
        compiler_params=_cparams(1),
        name="memory_kv",
    )(mem, wk.astype(BF16), wv.astype(BF16))


def _attn_kernel(x_ref, k_ref, v_ref, wq_ref, wo_ref, g_ref, b_ref, wrt_ref, brt_ref,
                 x2_ref, x2b_ref, tw_ref, lpos_ref, meta_ref, cnt_ref, cnt_scr, *, ts):
    @pl.when((pl.program_id(0) == 0) & (pl.program_id(1) == 0))
    def _():
        cnt_scr[...] = jnp.zeros(cnt_scr.shape, F32)

    x = x_ref[0]
    q = _dot(x.astype(BF16), wq_ref[...]).astype(BF16)
    heads = []
    for h in range(X_HEADS):
        hs = slice(h * X_HEAD_DIM, (h + 1) * X_HEAD_DIM)
        sc = _dot_nt(q[:, hs], k_ref[0][:, hs]) * (X_HEAD_DIM ** -0.5)
        e = jnp.exp(sc - jnp.max(sc, axis=-1, keepdims=True))
        p = e / jnp.sum(e, axis=-1, keepdims=True)
        heads.append(_dot(p.astype(BF16), v_ref[0][:, hs]))
    att = _dot(jnp.concatenate(heads, axis=1).astype(BF16), wo_ref[...])
    x2 = _layer_norm(DN_ALPHA * x + att, g_ref[...], b_ref[...])
    x2_ref[0] = x2

    x_hi, x_lo = _split_bf16(x2)
    x2b_ref[0] = x_hi
    w_hi, w_lo = _split_bf16(wrt_ref[...])
    logits = _dot_nt(w_hi, x_hi) + _dot_nt(w_hi, x_lo) + _dot_nt(w_lo, x_hi) + brt_ref[...]
    e_iota = lax.broadcasted_iota(I32, (N_EXPERTS, ts), 0)
    onehots, vals = [], []
    for _ in range(TOP_K):
        mx = jnp.max(logits, axis=0, keepdims=True)
        sel = jnp.min(jnp.where(logits == mx, e_iota, N_EXPERTS), axis=0, keepdims=True)
        oh = e_iota == sel
        onehots.append(oh)
        vals.append(mx)
        logits = jnp.where(oh, -jnp.inf, logits)
    exps = [jnp.exp(v - vals[0]) for v in vals]
    tot = exps[0] + exps[1] + exps[2] + exps[3]
    tw_ref[0] = jnp.concatenate([ex / tot for ex in exps], axis=0)

    member = jnp.where(onehots[0] | onehots[1] | onehots[2] | onehots[3], 1.0, 0.0)
    t_r = lax.broadcasted_iota(I32, (ts, ts), 0)
    t_c = lax.broadcasted_iota(I32, (ts, ts), 1)
    before = jnp.where(t_r < t_c, 1.0, 0.0).astype(BF16)
    pos = _dot(member.astype(BF16), before)
    n_e = jnp.sum(member, axis=1, keepdims=True)
    q_e = jnp.floor((n_e + (SEG_ALIGN - 1.0)) * (1.0 / SEG_ALIGN))
    e_r = lax.broadcasted_iota(I32, (N_EXPERTS, N_EXPERTS), 0)
    e_c = lax.broadcasted_iota(I32, (N_EXPERTS, N_EXPERTS), 1)
    lower = jnp.where(e_c < e_r, 1.0, 0.0).astype(BF16)
    q_wide = jnp.broadcast_to(q_e, (N_EXPERTS, 128)).astype(BF16)
    lo_e = SEG_ALIGN * _dot(lower, q_wide)[:, 0:1]
    p_e = SEG_ALIGN * q_e
    base = lo_e + pos
    lpos = [jnp.sum(jnp.where(oh, base, 0.0), axis=0, keepdims=True) for oh in onehots]
    lpos_ref[0] = jnp.concatenate(lpos, axis=0).astype(I32)
    lane = lax.broadcasted_iota(I32, (N_EXPERTS, 128), 1)
    taken = cnt_scr[...]
    meta = jnp.where(lane == 0, p_e, jnp.where(lane == 1, lo_e, jnp.where(lane == 2, taken, 0.0)))
    meta_ref[0] = meta.astype(I32)
    cnt_scr[...] = taken + p_e
    cnt_ref[...] = cnt_scr[...].astype(I32)


def _attn_router(x, k_mem, v_mem, wq, wo, g, b, w_router, b_router, *, ts):
    bsz, s, _ = x.shape
    n_mem = k_mem.shape[1]
    const2 = lambda bi, si: (0, 0)
    tok = lambda bi, si: (bi, si, 0)
    lanes = lambda bi, si: (bi, 0, si)
    return pl.pallas_call(
        functools.partial(_attn_kernel, ts=ts),
        grid=(bsz, s // ts),
        in_specs=[
            pl.BlockSpec((1, ts, D_MODEL), tok),
            pl.BlockSpec((1, n_mem, D_MODEL), lambda bi, si: (bi, 0, 0)),
            pl.BlockSpec((1, n_mem, D_MODEL), lambda bi, si: (bi, 0, 0)),
            pl.BlockSpec((D_MODEL, D_MODEL), const2),
            pl.BlockSpec((D_MODEL, D_MODEL), const2),
            pl.BlockSpec((1, D_MODEL), const2),
            pl.BlockSpec((1, D_MODEL), const2),
            pl.BlockSpec((N_EXPERTS, D_MODEL), const2),
            pl.BlockSpec((N_EXPERTS, 1), const2),
        ],
        out_specs=[
            pl.BlockSpec((1, ts, D_MODEL), tok),
            pl.BlockSpec((1, ts, D_MODEL), tok),
            pl.BlockSpec((1, TOP_K, ts), lanes),
            pl.BlockSpec((1, TOP_K, ts), lanes),
            pl.BlockSpec((1, N_EXPERTS, 128), lambda bi, si: (bi * (s // ts) + si, 0, 0)),
            pl.BlockSpec((N_EXPERTS, 128), const2),
        ],
        out_shape=[
            jax.ShapeDtypeStruct((bsz, s, D_MODEL), F32),
            jax.ShapeDtypeStruct((bsz, s, D_MODEL), BF16),
            jax.ShapeDtypeStruct((bsz, TOP_K, s), F32),
            jax.ShapeDtypeStruct((bsz, TOP_K, s), I32),
            jax.ShapeDtypeStruct((bsz * (s // ts), N_EXPERTS, 128), I32),
            jax.ShapeDtypeStruct((N_EXPERTS, 128), I32),
        ],
        scratch_shapes=[pltpu.VMEM((N_EXPERTS, 128), F32)],
        compiler_params=_cparams(2),
        name="memory_attention_router",
    )(x, k_mem, v_mem, wq.astype(BF16), wo.astype(BF16), g.reshape(1, -1), b.reshape(1, -1),
      w_router.T, b_router.reshape(-1, 1))


def _segment_copies(seg_ref, make_copy, start, *, tt):
    def per_expert(e, carry):
        n = seg_ref[0, 0, e]
        lo = seg_ref[0, 0, N_EXPERTS + e]
        dst = seg_ref[0, 0, 2 * N_EXPERTS + e]
        for bit in range(tt.bit_length() - 1, SEG_ALIGN.bit_length() - 2, -1):
            size = 1 << bit
            done = (n >> (bit + 1)) << (bit + 1)

            @pl.when((n & size) != 0)
            def _():
                cp = make_copy(pl.multiple_of(lo + done, SEG_ALIGN),
                               pl.multiple_of(dst + done, SEG_ALIGN), size)
                if start:
                    cp.start()
                else:
                    cp.wait()
        return carry

    lax.fori_loop(0, N_EXPERTS, per_expert, 0)


def _dispatch_kernel(last_ref, nused_ref, seg_ref, lpos_ref, x_ref, xs_hbm, sort_scr, zero_scr,
                     sem, zsem, *, tt, tm, n_tiles, n_sorted):
    @pl.when(pl.program_id(0) == 0)
    def _():
        zero_scr[...] = jnp.zeros(zero_scr.shape, U32)

        def zero_copy(row):
            return pltpu.make_async_copy(
                zero_scr, xs_hbm.at[pl.ds(pl.multiple_of(row, tm), tm), :], zsem)

        for e in range(N_EXPERTS):
            @pl.when(last_ref[e] >= 0)
            def _():
                zero_copy(last_ref[e]).start()

        def start_tail(i, carry):
            zero_copy(i * tm).start()
            return carry

        lax.fori_loop(nused_ref[0], n_tiles, start_tail, 0)

        for e in range(N_EXPERTS):
            @pl.when(last_ref[e] >= 0)
            def _():
                zero_copy(last_ref[e]).wait()

        def wait_tail(i, carry):
            zero_copy(i * tm).wait()
            return carry

        lax.fori_loop(nused_ref[0], n_tiles, wait_tail, 0)

    lp = lpos_ref[0]
    r_iota = lax.broadcasted_iota(I32, (n_sorted, tt), 0)
    perm = jnp.where(r_iota == lp[0:1, :], 1.0, 0.0)
    for k in range(1, TOP_K):
        perm = jnp.where(r_iota == lp[k:k + 1, :], 1.0, perm)
    perm = perm.astype(BF16)
    lo_bits = lax.bitcast_convert_type(_dot(perm, x_ref[:, 0:HALF]), U32)
    hi_bits = lax.bitcast_convert_type(_dot(perm, x_ref[:, HALF:]), U32)
    sort_scr[...] = (lo_bits >> 16) | (hi_bits & jnp.uint32(0xFFFF0000))

    def seg_copy(lo, dst, size):
        return pltpu.make_async_copy(sort_scr.at[pl.ds(lo, size), :],
                                     xs_hbm.at[pl.ds(dst, size), :], sem)

    _segment_copies(seg_ref, seg_copy, True, tt=tt)
    _segment_copies(seg_ref, seg_copy, False, tt=tt)


def _dispatch(x2b, lpos_tiles, segs, last_tile_row, n_used, n_rows, *, tt, tm, n_sorted):
    n_tok = x2b.shape[0]
    grid_spec = pltpu.PrefetchScalarGridSpec(
        num_scalar_prefetch=2,
        grid=(n_tok // tt,),
        in_specs=[
            pl.BlockSpec((1, 1, 3 * N_EXPERTS), lambda i, la, nu: (i, 0, 0),
                         memory_space=pltpu.SMEM),
            pl.BlockSpec((1, TOP_K, tt), lambda i, la, nu: (i, 0, 0)),
            pl.BlockSpec((tt, D_MODEL), lambda i, la, nu: (i, 0)),
        ],
        out_specs=pl.BlockSpec(memory_space=pl.ANY),
        scratch_shapes=[pltpu.VMEM((n_sorted, HALF), U32), pltpu.VMEM((tm, HALF), U32),
                        pltpu.SemaphoreType.DMA, pltpu.SemaphoreType.DMA],
    )
    return pl.pallas_call(
        functools.partial(_dispatch_kernel, tt=tt, tm=tm, n_tiles=n_rows // tm,
                          n_sorted=n_sorted),
        grid_spec=grid_spec,
        out_shape=jax.ShapeDtypeStruct((n_rows, HALF), U32),
        compiler_params=pltpu.CompilerParams(
            dimension_semantics=("arbitrary",), has_side_effects=True,
            vmem_limit_bytes=VMEM_LIMIT_BYTES),
        name="moe_dispatch",
    )(last_tile_row, n_used, segs, lpos_tiles, x2b)


def _ffn_kernel(te_ref, nused_ref, xs_ref, w1_ref, b1_ref, w2_ref, b2_ref, ys_ref):
    del te_ref

    @pl.when(pl.program_id(0) < nused_ref[0])
    def _():
        xt = _unpack_rows(xs_ref[...]).astype(BF16)
        gate = _dot(xt, w1_ref[0, :, 0:D_FF]) + b1_ref[0, :, 0:D_FF]
        up = _dot(xt, w1_ref[0, :, D_FF:]) + b1_ref[0, :, D_FF:]
        gate = jnp.minimum(gate, SWIGLU_LIMIT)
        up = jnp.clip(up, -SWIGLU_LIMIT, SWIGLU_LIMIT)
        hid = (up + 1.0) * gate * _sigmoid(SWIGLU_ALPHA * gate)
        ys_ref[...] = _pack_rows(_dot(hid.astype(BF16), w2_ref[0]) + b2_ref[0])

    @pl.when(pl.program_id(0) >= nused_ref[0])
    def _():
        ys_ref[...] = jnp.zeros(ys_ref.shape, U32)


def _grouped_ffn(xs, tile_expert, n_used, w1, b1, w2, b2, *, tm):
    n_rows = xs.shape[0]
    n_tiles = n_rows // tm
    row_blk = lambda i, te, nu: (jnp.minimum(i, nu[0] - 1), 0)
    exp_blk = lambda i, te, nu: (te[i], 0, 0)
    grid_spec = pltpu.PrefetchScalarGridSpec(
        num_scalar_prefetch=2,
        grid=(n_tiles,),
        in_specs=[
            pl.BlockSpec((tm, HALF), row_blk),
            pl.BlockSpec((1, D_MODEL, 2 * D_FF), exp_blk),
            pl.BlockSpec((1, 1, 2 * D_FF), exp_blk),
            pl.BlockSpec((1, D_FF, D_MODEL), exp_blk),
            pl.BlockSpec((1, 1, D_MODEL), exp_blk),
        ],
        out_specs=pl.BlockSpec((tm, HALF), lambda i, te, nu: (i, 0)),
    )
    return pl.pallas_call(
        _ffn_kernel,
        grid_spec=grid_spec,
        out_shape=jax.ShapeDtypeStruct((n_rows, HALF), U32),
        compiler_params=_cparams(1),
        name="moe_grouped_ffn",
    )(tile_expert, n_used, xs, w1, b1, w2, b2)


def _combine_kernel(seg_ref, x_ref, lpos_ref, tw_ref, g_ref, b_ref, ys_hbm, o_ref, rows_scr, sem,
                    *, tt, n_sorted):
    @pl.when(pl.program_id(0) == 0)
    def _():
        rows_scr[...] = jnp.zeros(rows_scr.shape, U32)

    def seg_copy(lo, src, size):
        return pltpu.make_async_copy(ys_hbm.at[pl.ds(src, size), :],
                                     rows_scr.at[pl.ds(lo, size), :], sem)

    _segment_copies(seg_ref, seg_copy, True, tt=tt)

    lp = lpos_ref[...]
    tw = tw_ref[...]
    c_iota = lax.broadcasted_iota(I32, (tt, n_sorted), 1)
    wmat = jnp.where(c_iota == lp[:, 0:1], tw[:, 0:1], 0.0)
    for k in range(1, TOP_K):
        wmat = jnp.where(c_iota == lp[:, k:k + 1], tw[:, k:k + 1], wmat)
    wmat = wmat.astype(BF16)

    _segment_copies(seg_ref, seg_copy, False, tt=tt)

    rows = rows_scr[...]
    r_lo = lax.bitcast_convert_type(rows << 16, F32).astype(BF16)
    r_hi = lax.bitcast_convert_type(rows & jnp.uint32(0xFFFF0000), F32).astype(BF16)
    y = jnp.concatenate([_dot(wmat, r_lo), _dot(wmat, r_hi)], axis=1)
    o_ref[...] = _layer_norm(DN_ALPHA * x_ref[...] + y, g_ref[...], b_ref[...])


def _combine(x2, ys, segs, lpos_tok, tw_tok, g, b, *, tt, n_sorted):
    n_tok = x2.shape[0]
    const2 = lambda i: (0, 0)
    return pl.pallas_call(
        functools.partial(_combine_kernel, tt=tt, n_sorted=n_sorted),
        grid=(n_tok // tt,),
        in_specs=[
            pl.BlockSpec((1, 1, 3 * N_EXPERTS), lambda i: (i, 0, 0), memory_space=pltpu.SMEM),
            pl.BlockSpec((tt, D_MODEL), lambda i: (i, 0)),
            pl.BlockSpec((tt, TOP_K), lambda i: (i, 0)),
            pl.BlockSpec((tt, TOP_K), lambda i: (i, 0)),
            pl.BlockSpec((1, D_MODEL), const2),
            pl.BlockSpec((1, D_MODEL), const2),
            pl.BlockSpec(memory_space=pl.ANY),
        ],
        out_specs=pl.BlockSpec((tt, D_MODEL), lambda i: (i, 0)),
        out_shape=jax.ShapeDtypeStruct((n_tok, D_MODEL), F32),
        scratch_shapes=[pltpu.VMEM((n_sorted, HALF), U32), pltpu.SemaphoreType.DMA],
        compiler_params=_cparams(1),
        name="moe_combine",
    )(segs, x2, lpos_tok, tw_tok, g.reshape(1, -1), b.reshape(1, -1), ys)


def _moe_layer(x2, x2b, tw, lpos, meta, counts, w1, b1, w2, b2, g, b, layer, *, tm, tt):
    bsz, s, _ = x2.shape
    n_tok = bsz * s
    n_tok_tiles = n_tok // tt
    seg_pad = N_EXPERTS * (SEG_ALIGN - 1)
    n_sorted = -(-(TOP_K * tt + seg_pad) // MXU_DIM) * MXU_DIM
    n_tiles = -(-(n_tok * TOP_K + n_tok_tiles * seg_pad) // tm) + N_EXPERTS
    n_rows = n_tiles * tm

    tiles_e = (counts + tm - 1) // tm
    tile_end = jnp.cumsum(tiles_e)
    start_e = (tile_end - tiles_e) * tm
    n_used = tile_end[-1]
    last_tile_row = jnp.where(tiles_e > 0, (tile_end - 1) * tm, -1).astype(I32)
    n_used_arr = n_used.reshape(1).astype(I32)
    tile_ids = jnp.minimum(jnp.arange(n_tiles, dtype=I32), n_used - 1)
    tile_expert = jnp.sum(tile_ids[:, None] >= tile_end[None, :], axis=1).astype(I32)
    segs = jnp.concatenate([meta[:, :, 0], meta[:, :, 1], meta[:, :, 2] + start_e[None, :]], axis=1)
    segs = segs.reshape(n_tok_tiles, 1, 3 * N_EXPERTS).astype(I32)
    lpos_tiles = lpos.reshape(bsz, TOP_K, s // tt, tt).transpose(0, 2, 1, 3)
    lpos_tiles = lpos_tiles.reshape(n_tok_tiles, TOP_K, tt)

    xs = _dispatch(x2b.reshape(n_tok, D_MODEL), lpos_tiles, segs, last_tile_row, n_used_arr,
                   n_rows, tt=tt, tm=tm, n_sorted=n_sorted)
    ys = _grouped_ffn(xs, tile_expert + layer * N_EXPERTS, n_used_arr, w1, b1, w2, b2, tm=tm)
    lpos_tok = lpos.transpose(0, 2, 1).reshape(n_tok, TOP_K)
    tw_tok = tw.transpose(0, 2, 1).reshape(n_tok, TOP_K)
    out = _combine(x2.reshape(n_tok, D_MODEL), ys, segs, lpos_tok, tw_tok, g, b, tt=tt,
                   n_sorted=n_sorted)
    return out.reshape(bsz, s, D_MODEL)


def _pick_tile(n, want):
    t = min(n, want)
    while n % t:
        t //= 2
    return t


def kernel(x, mem, ev_w_in, ev_conv_a, ev_conv_b, ev_conv_b_bias, ev_w_rgate, ev_b_rgate, ev_w_igate, ev_b_igate, ev_lambda, ev_w_out, od_w_in, od_b_gates, od_norm_g, od_w_out, xa_wq, xa_wk, xa_wv, xa_wo, moe_w_router, moe_b_router, moe_w1, moe_b1, moe_w2, moe_b2, ln_g, ln_b):
    bsz, s, _ = x.shape
    ts_even = _pick_tile(s, 512)
    ts_odd = _pick_tile(s, 256)
    ts_attn = _pick_tile(s, 512)
    tm = 512

    w1_bf = moe_w1.astype(BF16).reshape(DEPTH * N_EXPERTS, D_MODEL, 2 * D_FF)
    w2_bf = moe_w2.astype(BF16).reshape(DEPTH * N_EXPERTS, D_FF, D_MODEL)
    b1_all = moe_b1.reshape(DEPTH * N_EXPERTS, 1, 2 * D_FF)
    b2_all = moe_b2.reshape(DEPTH * N_EXPERTS, 1, D_MODEL)
    for layer in range(DEPTH):
        j = layer // 2
        if layer % 2 == 0:
            x = _even_layer(x, ev_w_in[j], ev_conv_a[j], ev_conv_b[j], ev_conv_b_bias[j],
                            ev_w_rgate[j], ev_b_rgate[j], ev_w_igate[j], ev_b_igate[j],
                            ev_lambda[j], ev_w_out[j], ln_g[layer, 0], ln_b[layer, 0], ts=ts_even)
        else:
            x = _odd_layer(x, od_w_in[j], od_b_gates[j], od_norm_g[j], od_w_out[j],
                           ln_g[layer, 0], ln_b[layer, 0], ts=ts_odd)
        k_mem, v_mem = _kv_proj(mem, xa_wk[layer], xa_wv[layer])
        x2, x2b, tw, lpos, meta, cnt = _attn_router(
            x, k_mem, v_mem, xa_wq[layer], xa_wo[layer], ln_g[layer, 1], ln_b[layer, 1],
            moe_w_router[layer], moe_b_router[layer], ts=ts_attn)
        x = _moe_layer(x2, x2b, tw, lpos, meta, cnt[:, 0], w1_bf, b1_all, w2_bf, b2_all,
                       ln_g[layer, 2], ln_b[layer, 2], layer, tm=tm, tt=ts_attn)
    return x
```

```python
import functools

import jax
import jax.numpy as jnp
from jax import lax
from jax.experimental import pallas as pl
from jax.experimental.pallas import tpu as pltpu

F32 = jnp.float32
BF16 = jnp.bfloat16
U32 = jnp.uint32
I32 = jnp.int32

D_MODEL = 1024
DEPTH = 4
A_WIDTH = 512
B_WIDTH = 1024
B_HEADS = 8
B_HEAD_DIM = 128
LRU_C = 8.0
M_HEADS = 4
M_QK_DIM = 128
M_V_DIM = 256
M_QK = 512
M_V = 1024
M_CHUNK = 128
X_HEADS = 4
X_HEAD_DIM = 256
N_EXPERTS = 32
TOP_K = 4
D_FF = 1024
SWIGLU_LIMIT = 7.0
SWIGLU_ALPHA = 1.702
DN_ALPHA = (2 * DEPTH) ** 0.25
LN_EPS = 1e-5
RMS_EPS = 1e-6
HALF = D_MODEL // 2
SEG_ALIGN = 8
MXU_DIM = 256

VMEM_LIMIT_BYTES = 56 * 1024 * 1024


def _cparams(n_grid):
    return pltpu.CompilerParams(
        dimension_semantics=("arbitrary",) * n_grid, vmem_limit_bytes=VMEM_LIMIT_BYTES)


def _layer_norm(y, g, b):
    mu = jnp.mean(y, axis=-1, keepdims=True)
    yc = y - mu
    var = jnp.mean(yc * yc, axis=-1, keepdims=True)
    return yc * lax.rsqrt(var + LN_EPS) * g + b


def _sigmoid(x):
    return 1.0 / (1.0 + jnp.exp(-x))


def _softplus(x):
    return jnp.maximum(x, 0.0) + jnp.log1p(jnp.exp(-jnp.abs(x)))


def _gelu_tanh(x):
    return 0.5 * x * (1.0 + jnp.tanh(0.7978845608028654 * (x + 0.044715 * (x * x * x))))


def _split_bf16(x):
    hi = x.astype(BF16)
    lo = (x - hi.astype(F32)).astype(BF16)
    return hi, lo


def _dot(a, b):
    return jnp.dot(a, b, preferred_element_type=F32)


def _dot_nt(a, b):
    return lax.dot_general(a, b, (((1,), (1,)), ((), ())), preferred_element_type=F32)


def _dot_tn(a, b):
    return lax.dot_general(a, b, (((0,), (0,)), ((), ())), preferred_element_type=F32)


def _pack_rows(y):
    bits = lax.bitcast_convert_type(y.astype(BF16).astype(F32), U32)
    return (bits[:, :HALF] >> 16) | (bits[:, HALF:] & jnp.uint32(0xFFFF0000))


def _unpack_rows(w):
    lo = lax.bitcast_convert_type(w << 16, F32)
    hi = lax.bitcast_convert_type(w & jnp.uint32(0xFFFF0000), F32)
    return jnp.concatenate([lo, hi], axis=1)


def _even_kernel(x_ref, win_ref, ca_ref, cb_ref, cbb_ref, wr_ref, br_ref, wi_ref, bi_ref, lam_ref,
                 wout_ref, g_ref, b_ref, o_ref,
                 av_scr, bu_scr, a_scr, b_scr, h_scr, carry_scr, *, ts):
    @pl.when(pl.program_id(1) == 0)
    def _():
        av_scr[0:8, :] = jnp.zeros((8, A_WIDTH), F32)
        bu_scr[0:8, :] = jnp.zeros((8, B_WIDTH), F32)
        carry_scr[...] = jnp.zeros((8, B_WIDTH), F32)

    x = x_ref[0]
    z = _dot(x.astype(BF16), win_ref[...])
    a_b = z[:, 0:A_WIDTH]
    a_c = z[:, A_WIDTH:2 * A_WIDTH]
    a_x = z[:, 2 * A_WIDTH:3 * A_WIDTH]
    b_u = z[:, 3 * A_WIDTH:3 * A_WIDTH + B_WIDTH]
    b_g = z[:, 3 * A_WIDTH + B_WIDTH:]

    av_scr[8:8 + ts, :] = a_c * a_x
    ca = ca_ref[...]
    conv_a = (ca[2:3, :] * av_scr[8:8 + ts, :] + ca[1:2, :] * av_scr[7:7 + ts, :]
              + ca[0:1, :] * av_scr[6:6 + ts, :])
    y_a = a_b * conv_a
    av_scr[0:8, :] = av_scr[ts:ts + 8, :]

    bu_scr[8:8 + ts, :] = b_u
    cb = cb_ref[...]
    u = (cb[3:4, :] * bu_scr[8:8 + ts, :] + cb[2:3, :] * bu_scr[7:7 + ts, :]
         + cb[1:2, :] * bu_scr[6:6 + ts, :] + cb[0:1, :] * bu_scr[5:5 + ts, :] + cbb_ref[...])
    bu_scr[0:8, :] = bu_scr[ts:ts + 8, :]

    ub = u.astype(BF16)
    r_parts, i_parts = [], []
    for h in range(B_HEADS):
        uh = ub[:, h * B_HEAD_DIM:(h + 1) * B_HEAD_DIM]
        r_parts.append(_dot(uh, wr_ref[h]))
        i_parts.append(_dot(uh, wi_ref[h]))
    r = _sigmoid(jnp.concatenate(r_parts, axis=1) + br_ref[...])
    ig = _sigmoid(jnp.concatenate(i_parts, axis=1) + bi_ref[...])
    log_a = (-LRU_C) * r * _softplus(-lam_ref[...])
    a = jnp.exp(log_a)
    a_scr[...] = a
    b_scr[...] = jnp.sqrt(1.0 - a * a) * (ig * u)

    row8 = lax.broadcasted_iota(I32, (8, B_WIDTH), 0)

    def group(i, carry):
        r0 = pl.multiple_of(i * 8, 8)
        ga = a_scr[pl.ds(r0, 8), :]
        gb = b_scr[pl.ds(r0, 8), :]
        for d in (1, 2, 4):
            keep = row8 >= d
            a_sh = jnp.where(keep, pltpu.roll(ga, d, 0), 1.0)
            b_sh = jnp.where(keep, pltpu.roll(gb, d, 0), 0.0)
            gb = ga * b_sh + gb
            ga = ga * a_sh
        hg = gb + ga * carry
        h_scr[pl.ds(r0, 8), :] = hg
        return jnp.broadcast_to(hg[7:8, :], (8, B_WIDTH))

    carry_scr[...] = lax.fori_loop(0, ts // 8, group, carry_scr[...], unroll=4)

    y_b = _gelu_tanh(b_g) * h_scr[...]
    mix = (_dot(y_a.astype(BF16), wout_ref[0:A_WIDTH, :])
           + _dot(y_b.astype(BF16), wout_ref[A_WIDTH:, :]))
    o_ref[0] = _layer_norm(DN_ALPHA * x + mix, g_ref[...], b_ref[...])


def _even_layer(x, w_in, conv_a, conv_b, conv_b_bias, w_r, b_r, w_i, b_i, lam, w_out, g, b, *, ts):
    bsz, s, _ = x.shape
    ev_in = 3 * A_WIDTH + 2 * B_WIDTH
    const2 = lambda bi, si: (0, 0)
    const3 = lambda bi, si: (0, 0, 0)
    return pl.pallas_call(
        functools.partial(_even_kernel, ts=ts),
        grid=(bsz, s // ts),
        in_specs=[
            pl.BlockSpec((1, ts, D_MODEL), lambda bi, si: (bi, si, 0)),
            pl.BlockSpec((D_MODEL, ev_in), const2),
            pl.BlockSpec((3, A_WIDTH), const2),
            pl.BlockSpec((4, B_WIDTH), const2),
            pl.BlockSpec((1, B_WIDTH), const2),
            pl.BlockSpec((B_HEADS, B_HEAD_DIM, B_HEAD_DIM), const3),
            pl.BlockSpec((1, B_WIDTH), const2),
            pl.BlockSpec((B_HEADS, B_HEAD_DIM, B_HEAD_DIM), const3),
            pl.BlockSpec((1, B_WIDTH), const2),
            pl.BlockSpec((1, B_WIDTH), const2),
            pl.BlockSpec((A_WIDTH + B_WIDTH, D_MODEL), const2),
            pl.BlockSpec((1, D_MODEL), const2),
            pl.BlockSpec((1, D_MODEL), const2),
        ],
        out_specs=pl.BlockSpec((1, ts, D_MODEL), lambda bi, si: (bi, si, 0)),
        out_shape=jax.ShapeDtypeStruct(x.shape, F32),
        scratch_shapes=[
            pltpu.VMEM((ts + 8, A_WIDTH), F32),
            pltpu.VMEM((ts + 8, B_WIDTH), F32),
            pltpu.VMEM((ts, B_WIDTH), F32),
            pltpu.VMEM((ts, B_WIDTH), F32),
            pltpu.VMEM((ts, B_WIDTH), F32),
            pltpu.VMEM((8, B_WIDTH), F32),
        ],
        compiler_params=_cparams(2),
        name="even_mixer",
    )(x, w_in.astype(BF16), conv_a, conv_b, conv_b_bias.reshape(1, -1), w_r.astype(BF16),
      b_r.reshape(1, -1), w_i.astype(BF16), b_i.reshape(1, -1), lam.reshape(1, -1),
      w_out.astype(BF16), g.reshape(1, -1), b.reshape(1, -1))


GATE_PAD = 128
V_EXT = M_V_DIM + 128


def _odd_kernel(x_ref, win_ref, wg_ref, bgc_ref, ng_ref, wout_ref, g_ref, b_ref,
                o_ref, c_scr, m_scr, h_scr, *, ts):
    L = M_CHUNK

    @pl.when(pl.program_id(1) == 0)
    def _():
        c_scr[...] = jnp.zeros(c_scr.shape, F32)
        m_scr[...] = jnp.zeros(m_scr.shape, F32)

    x = x_ref[0]
    x_hi, x_lo = _split_bf16(x)
    z = _dot(x_hi, win_ref[...])
    q_all = (z[:, 0:M_QK] * (M_QK_DIM ** -0.5)).astype(BF16)
    k_all = z[:, M_QK:2 * M_QK].astype(BF16)
    v_all = z[:, 2 * M_QK:2 * M_QK + M_V].astype(BF16)
    o_all = z[:, 2 * M_QK + M_V:]

    wg_hi, wg_lo = _split_bf16(wg_ref[...])
    gates_c = _dot(x_hi, wg_hi) + _dot(x_lo, wg_hi) + _dot(x_hi, wg_lo) + bgc_ref[...]
    logf_c = -_softplus(-gates_c)

    rows = lax.broadcasted_iota(I32, (L, L), 0)
    cols = lax.broadcasted_iota(I32, (L, L), 1)
    causal = rows >= cols
    tril = jnp.where(causal, 1.0, 0.0).astype(BF16)
    lane0 = jnp.where(lax.broadcasted_iota(I32, (L, 128), 1) == 0, 1.0, 0.0).astype(BF16)

    for c in range(ts // L):
        sl = slice(c * L, (c + 1) * L)
        fc_hi, fc_lo = _split_bf16(logf_c[sl, :])
        fc_lo2 = (logf_c[sl, :] - fc_hi.astype(F32) - fc_lo.astype(F32)).astype(BF16)
        bcum_c = _dot(tril, fc_hi) + _dot(tril, fc_lo) + _dot(tril, fc_lo2)
        i_minus_b = gates_c[sl, :] - pltpu.roll(bcum_c, GATE_PAD - M_HEADS, 1)
        i_minus_b_t = i_minus_b.T
        for h in range(M_HEADS):
            m_st = m_scr[h][0:1, 0:1]
            b_col = bcum_c[:, M_HEADS + h:M_HEADS + h + 1]
            i_col = gates_c[sl, h:h + 1]
            b_end = b_col[L - 1:L, :]

            d = jnp.where(causal, b_col + i_minus_b_t[h:h + 1, :], -jnp.inf)
            inter = b_col + m_st
            m_t = jnp.maximum(inter, jnp.max(d, axis=1, keepdims=True))
            p = jnp.exp(d - m_t)
            w_inter = jnp.exp(inter - m_t)

            qh = q_all[sl, h * M_QK_DIM:(h + 1) * M_QK_DIM]
            kh = k_all[sl, h * M_QK_DIM:(h + 1) * M_QK_DIM]
            v_ext = jnp.concatenate([v_all[sl, h * M_V_DIM:(h + 1) * M_V_DIM], lane0], axis=1)
            qk = (_dot_nt(qh, kh) * p).astype(BF16)
            c_ext = c_scr[h]
            nd = _dot(qk, v_ext) + w_inter * _dot(qh, c_ext.astype(BF16))
            den = nd[:, M_V_DIM:M_V_DIM + 1]
            hh = nd[:, 0:M_V_DIM] / jnp.maximum(jnp.abs(den), jnp.exp(-m_t))

            g_col = b_end - b_col + i_col
            m_new = jnp.maximum(b_end + m_st, jnp.max(g_col, axis=0, keepdims=True))
            wg_col = jnp.exp(g_col - m_new)
            decay = jnp.exp(b_end + m_st - m_new)
            kv = _dot_tn(kh, (wg_col * v_ext.astype(F32)).astype(BF16))
            c_scr[h] = decay * c_ext + kv
            m_scr[h] = jnp.broadcast_to(m_new, (8, 128))

            hn = hh * lax.rsqrt(jnp.mean(hh * hh, axis=-1, keepdims=True) + RMS_EPS)
            h_scr[sl, h * M_V_DIM:(h + 1) * M_V_DIM] = hn

    gated = _sigmoid(o_all) * (h_scr[...] * ng_ref[...])
    mix = _dot(gated.astype(BF16), wout_ref[...])
    o_ref[0] = _layer_norm(DN_ALPHA * x + mix, g_ref[...], b_ref[...])


def _odd_layer(x, w_in, b_gates, norm_g, w_out, g, b, *, ts):
    bsz, s, _ = x.shape
    n_main = 2 * M_QK + 2 * M_V
    w_main = w_in[:, :n_main].astype(BF16)
    w_gate = w_in[:, n_main:]
    wg_pad = jnp.pad(w_gate, ((0, 0), (0, GATE_PAD - 2 * M_HEADS)))
    bg_col = jnp.pad(b_gates, (0, GATE_PAD - 2 * M_HEADS)).reshape(1, GATE_PAD)
    const2 = lambda bi, si: (0, 0)
    return pl.pallas_call(
        functools.partial(_odd_kernel, ts=ts),
        grid=(bsz, s // ts),
        in_specs=[
            pl.BlockSpec((1, ts, D_MODEL), lambda bi, si: (bi, si, 0)),
            pl.BlockSpec((D_MODEL, n_main), const2),
            pl.BlockSpec((D_MODEL, GATE_PAD), const2),
            pl.BlockSpec((1, GATE_PAD), const2),
            pl.BlockSpec((1, M_V), const2),
            pl.BlockSpec((M_V, D_MODEL), const2),
            pl.BlockSpec((1, D_MODEL), const2),
            pl.BlockSpec((1, D_MODEL), const2),
        ],
        out_specs=pl.BlockSpec((1, ts, D_MODEL), lambda bi, si: (bi, si, 0)),
        out_shape=jax.ShapeDtypeStruct(x.shape, F32),
        scratch_shapes=[
            pltpu.VMEM((M_HEADS, M_QK_DIM, V_EXT), F32),
            pltpu.VMEM((M_HEADS, 8, 128), F32),
            pltpu.VMEM((ts, M_V), F32),
        ],
        compiler_params=_cparams(2),
        name="odd_mixer",
    )(x, w_main, wg_pad, bg_col, norm_g.reshape(1, -1), w_out.astype(BF16),
      g.reshape(1, -1), b.reshape(1, -1))


def _kv_kernel(mem_ref, wk_ref, wv_ref, k_ref, v_ref):
    m = mem_ref[0].astype(BF16)
    k_ref[0] = _dot(m, wk_ref[...]).astype(BF16)
    v_ref[0] = _dot(m, wv_ref[...]).astype(BF16)


def _kv_proj(mem, wk, wv):
    bsz, n_mem, _ = mem.shape
    const2 = lambda bi: (0, 0)
    blk = pl.BlockSpec((1, n_mem, D_MODEL), lambda bi: (bi, 0, 0))
    return pl.pallas_call(
        _kv_kernel,
        grid=(bsz,),
        in_specs=[blk, pl.BlockSpec((D_MODEL, D_MODEL), const2),
                  pl.BlockSpec((D_MODEL, D_MODEL), const2)],
        out_specs=[blk, blk],
        out_shape=[jax.ShapeDtypeStruct(mem.shape, BF16)] * 2,
        compiler_params=_cparams(1),
        name="memory_kv",
    )(mem, wk.astype(BF16), wv.astype(BF16))


def _attn_tile(x, k, v, wq_ref, wo_ref, g_ref, b_ref, wrt_ref, brt_ref, taken):
    ts = x.shape[0]
    q = _dot(x.astype(BF16), wq_ref[...]).astype(BF16)
    heads = []
    for h in range(X_HEADS):
        hs = slice(h * X_HEAD_DIM, (h + 1) * X_HEAD_DIM)
        sc = _dot_nt(q[:, hs], k[:, hs]) * (X_HEAD_DIM ** -0.5)
        e = jnp.exp(sc - jnp.max(sc, axis=-1, keepdims=True))
        p = e / jnp.sum(e, axis=-1, keepdims=True)
        heads.append(_dot(p.astype(BF16), v[:, hs]))
    att = _dot(jnp.concatenate(heads, axis=1).astype(BF16), wo_ref[...])
    x2 = _layer_norm(DN_ALPHA * x + att, g_ref[...], b_ref[...])

    x_hi, x_lo = _split_bf16(x2)
    w_hi, w_lo = _split_bf16(wrt_ref[...])
    logits = _dot_nt(w_hi, x_hi) + _dot_nt(w_hi, x_lo) + _dot_nt(w_lo, x_hi) + brt_ref[...]
    e_iota = lax.broadcasted_iota(I32, (N_EXPERTS, ts), 0)
    onehots, vals = [], []
    for _ in range(TOP_K):
        mx = jnp.max(logits, axis=0, keepdims=True)
        sel = jnp.min(jnp.where(logits == mx, e_iota, N_EXPERTS), axis=0, keepdims=True)
        oh = e_iota == sel
        onehots.append(oh)
        vals.append(mx)
        logits = jnp.where(oh, -jnp.inf, logits)
    exps = [jnp.exp(val - vals[0]) for val in vals]
    tot = exps[0] + exps[1] + exps[2] + exps[3]
    tw = jnp.concatenate([ex / tot for ex in exps], axis=0)

    member = jnp.where(onehots[0] | onehots[1] | onehots[2] | onehots[3], 1.0, 0.0)
    t_r = lax.broadcasted_iota(I32, (ts, ts), 0)
    t_c = lax.broadcasted_iota(I32, (ts, ts), 1)
    before = jnp.where(t_r < t_c, 1.0, 0.0).astype(BF16)
    pos = _dot(member.astype(BF16), before)
    n_e = jnp.sum(member, axis=1, keepdims=True)
    q_e = jnp.floor((n_e + (SEG_ALIGN - 1.0)) * (1.0 / SEG_ALIGN))
    e_r = lax.broadcasted_iota(I32, (N_EXPERTS, N_EXPERTS), 0)
    e_c = lax.broadcasted_iota(I32, (N_EXPERTS, N_EXPERTS), 1)
    lower = jnp.where(e_c < e_r, 1.0, 0.0).astype(BF16)
    q_wide = jnp.broadcast_to(q_e, (N_EXPERTS, 128)).astype(BF16)
    lo_e = SEG_ALIGN * _dot(lower, q_wide)[:, 0:1]
    p_e = SEG_ALIGN * q_e
    base = lo_e + pos
    lpos = [jnp.sum(jnp.where(oh, base, 0.0), axis=0, keepdims=True) for oh in onehots]
    lpos = jnp.concatenate(lpos, axis=0).astype(I32)
    lane = lax.broadcasted_iota(I32, (N_EXPERTS, 128), 1)
    meta = jnp.where(lane == 0, p_e, jnp.where(lane == 1, lo_e, jnp.where(lane == 2, taken, 0.0)))
    return x2, x_hi, tw, lpos, meta.astype(I32), taken + p_e


def _attn_kernel(x_ref, k_ref, v_ref, wq_ref, wo_ref, g_ref, b_ref, wrt_ref, brt_ref,
                 x2_ref, x2b_ref, tw_ref, lpos_ref, meta_ref, cnt_ref, cnt_scr, *, ts, n_sub):
    @pl.when((pl.program_id(0) == 0) & (pl.program_id(1) == 0))
    def _():
        cnt_scr[...] = jnp.zeros(cnt_scr.shape, F32)

    taken = cnt_scr[...]
    for u in range(n_sub):
        rs = slice(u * ts, (u + 1) * ts)
        x2, x_hi, tw, lpos, meta, taken = _attn_tile(
            x_ref[0, rs, :], k_ref[0], v_ref[0], wq_ref, wo_ref, g_ref, b_ref, wrt_ref, brt_ref,
            taken)
        x2_ref[0, rs, :] = x2
        x2b_ref[0, rs, :] = x_hi
        tw_ref[0, :, rs] = tw
        lpos_ref[0, :, rs] = lpos
        meta_ref[u] = meta
    cnt_scr[...] = taken
    cnt_ref[...] = taken.astype(I32)


def _attn_router(x, k_mem, v_mem, wq, wo, g, b, w_router, b_router, *, ts, n_sub):
    bsz, s, _ = x.shape
    n_mem = k_mem.shape[1]
    tg = ts * n_sub
    const2 = lambda bi, si: (0, 0)
    tok = lambda bi, si: (bi, si, 0)
    lanes = lambda bi, si: (bi, 0, si)
    return pl.pallas_call(
        functools.partial(_attn_kernel, ts=ts, n_sub=n_sub),
        grid=(bsz, s // tg),
        in_specs=[
            pl.BlockSpec((1, tg, D_MODEL), tok),
            pl.BlockSpec((1, n_mem, D_MODEL), lambda bi, si: (bi, 0, 0)),
            pl.BlockSpec((1, n_mem, D_MODEL), lambda bi, si: (bi, 0, 0)),
            pl.BlockSpec((D_MODEL, D_MODEL), const2),
            pl.BlockSpec((D_MODEL, D_MODEL), const2),
            pl.BlockSpec((1, D_MODEL), const2),
            pl.BlockSpec((1, D_MODEL), const2),
            pl.BlockSpec((N_EXPERTS, D_MODEL), const2),
            pl.BlockSpec((N_EXPERTS, 1), const2),
        ],
        out_specs=[
            pl.BlockSpec((1, tg, D_MODEL), tok),
            pl.BlockSpec((1, tg, D_MODEL), tok),
            pl.BlockSpec((1, TOP_K, tg), lanes),
            pl.BlockSpec((1, TOP_K, tg), lanes),
            pl.BlockSpec((n_sub, N_EXPERTS, 128), lambda bi, si: (bi * (s // tg) + si, 0, 0)),
            pl.BlockSpec((N_EXPERTS, 128), const2),
        ],
        out_shape=[
            jax.ShapeDtypeStruct((bsz, s, D_MODEL), F32),
            jax.ShapeDtypeStruct((bsz, s, D_MODEL), BF16),
            jax.ShapeDtypeStruct((bsz, TOP_K, s), F32),
            jax.ShapeDtypeStruct((bsz, TOP_K, s), I32),
            jax.ShapeDtypeStruct((bsz * (s // ts), N_EXPERTS, 128), I32),
            jax.ShapeDtypeStruct((N_EXPERTS, 128), I32),
        ],
        scratch_shapes=[pltpu.VMEM((N_EXPERTS, 128), F32)],
        compiler_params=_cparams(2),
        name="memory_attention_router",
    )(x, k_mem, v_mem, wq.astype(BF16), wo.astype(BF16), g.reshape(1, -1), b.reshape(1, -1),
      w_router.T, b_router.reshape(-1, 1))


SEG_CHUNK = 128
SEG_WIDTH = 4 * N_EXPERTS


def _segment_starts(seg_ref, make_copy):
    chunk_bit = SEG_CHUNK.bit_length() - 1

    def per_expert(e, carry):
        n = seg_ref[0, 0, e]
        lo = seg_ref[0, 0, N_EXPERTS + e]
        dst = seg_ref[0, 0, 2 * N_EXPERTS + e]

        def chunk(j, c):
            off = pl.multiple_of(j * SEG_CHUNK, SEG_CHUNK)
            make_copy(pl.multiple_of(lo + off, SEG_ALIGN), pl.multiple_of(dst + off, SEG_ALIGN),
                      SEG_CHUNK).start()
            return c

        lax.fori_loop(0, n >> chunk_bit, chunk, 0)
        for bit in range(chunk_bit - 1, SEG_ALIGN.bit_length() - 2, -1):
            size = 1 << bit
            done = (n >> (bit + 1)) << (bit + 1)

            @pl.when((n & size) != 0)
            def _():
                make_copy(pl.multiple_of(lo + done, SEG_ALIGN),
                          pl.multiple_of(dst + done, SEG_ALIGN), size).start()
        return carry

    lax.fori_loop(0, N_EXPERTS, per_expert, 0)


def _segment_waits(seg_ref, make_copy, *, n_sorted):
    total = seg_ref[0, 0, 3 * N_EXPERTS]
    for bit in range(n_sorted.bit_length() - 1, SEG_ALIGN.bit_length() - 2, -1):
        size = 1 << bit

        @pl.when((total & size) != 0)
        def _():
            make_copy(0, 0, size).wait()


def _dispatch_kernel(last_ref, nused_ref, seg_ref, lpos_ref, x_ref, xs_hbm, sort_scr, zero_scr,
                     sem, zsem, *, tt, tm, n_tiles, n_sorted):
    @pl.when(pl.program_id(0) == 0)
    def _():
        zero_scr[...] = jnp.zeros(zero_scr.shape, U32)

        def zero_copy(row):
            return pltpu.make_async_copy(
                zero_scr, xs_hbm.at[pl.ds(pl.multiple_of(row, tm), tm), :], zsem)

        for e in range(N_EXPERTS):
            @pl.when(last_ref[e] >= 0)
            def _():
                zero_copy(last_ref[e]).start()

        def start_tail(i, carry):
            zero_copy(i * tm).start()
            return carry

        lax.fori_loop(nused_ref[0], n_tiles, start_tail, 0)

        for e in range(N_EXPERTS):
            @pl.when(last_ref[e] >= 0)
            def _():
                zero_copy(last_ref[e]).wait()

        def wait_tail(i, carry):
            zero_copy(i * tm).wait()
            return carry

        lax.fori_loop(nused_ref[0], n_tiles, wait_tail, 0)

    lp = lpos_ref[0]
    r_iota = lax.broadcasted_iota(I32, (n_sorted, tt), 0)
    perm = jnp.where(r_iota == lp[0:1, :], 1.0, 0.0)
    for k in range(1, TOP_K):
        perm = jnp.where(r_iota == lp[k:k + 1, :], 1.0, perm)
    perm = perm.astype(BF16)
    lo_bits = lax.bitcast_convert_type(_dot(perm, x_ref[:, 0:HALF]), U32)
    hi_bits = lax.bitcast_convert_type(_dot(perm, x_ref[:, HALF:]), U32)
    sort_scr[...] = (lo_bits >> 16) | (hi_bits & jnp.uint32(0xFFFF0000))

    def seg_copy(lo, dst, size):
        return pltpu.make_async_copy(sort_scr.at[pl.ds(lo, size), :],
                                     xs_hbm.at[pl.ds(dst, size), :], sem)

    _segment_starts(seg_ref, seg_copy)
    _segment_waits(seg_ref, seg_copy, n_sorted=n_sorted)


def _dispatch(x2b, lpos_tiles, segs, last_tile_row, n_used, n_rows, *, tt, tm, n_sorted):
    n_tok = x2b.shape[0]
    grid_spec = pltpu.PrefetchScalarGridSpec(
        num_scalar_prefetch=2,
        grid=(n_tok // tt,),
        in_specs=[
            pl.BlockSpec((1, 1, SEG_WIDTH), lambda i, la, nu: (i, 0, 0),
                         memory_space=pltpu.SMEM),
            pl.BlockSpec((1, TOP_K, tt), lambda i, la, nu: (i, 0, 0)),
            pl.BlockSpec((tt, D_MODEL), lambda i, la, nu: (i, 0)),
        ],
        out_specs=pl.BlockSpec(memory_space=pl.ANY),
        scratch_shapes=[pltpu.VMEM((n_sorted, HALF), U32), pltpu.VMEM((tm, HALF), U32),
                        pltpu.SemaphoreType.DMA, pltpu.SemaphoreType.DMA],
    )
    return pl.pallas_call(
        functools.partial(_dispatch_kernel, tt=tt, tm=tm, n_tiles=n_rows // tm,
                          n_sorted=n_sorted),
        grid_spec=grid_spec,
        out_shape=jax.ShapeDtypeStruct((n_rows, HALF), U32),
        compiler_params=pltpu.CompilerParams(
            dimension_semantics=("arbitrary",), has_side_effects=True,
            vmem_limit_bytes=VMEM_LIMIT_BYTES),
        name="moe_dispatch",
    )(last_tile_row, n_used, segs, lpos_tiles, x2b)


def _ffn_kernel(te_ref, nused_ref, xs_ref, w1_ref, b1_ref, w2_ref, b2_ref, ys_ref, w1_scr, w2_scr):
    i = pl.program_id(0)

    @pl.when((i == 0) | (te_ref[i] != te_ref[jnp.maximum(i - 1, 0)]))
    def _():
        w1_scr[...] = w1_ref[0].astype(BF16)
        w2_scr[...] = w2_ref[0].astype(BF16)

    @pl.when(i < nused_ref[0])
    def _():
        xt = _unpack_rows(xs_ref[...]).astype(BF16)
        gate = _dot(xt, w1_scr[:, 0:D_FF]) + b1_ref[0, :, 0:D_FF]
        up = _dot(xt, w1_scr[:, D_FF:]) + b1_ref[0, :, D_FF:]
        gate = jnp.minimum(gate, SWIGLU_LIMIT)
        up = jnp.clip(up, -SWIGLU_LIMIT, SWIGLU_LIMIT)
        hid = (up + 1.0) * gate * _sigmoid(SWIGLU_ALPHA * gate)
        ys_ref[...] = _pack_rows(_dot(hid.astype(BF16), w2_scr[...]) + b2_ref[0])

    @pl.when(i >= nused_ref[0])
    def _():
        ys_ref[...] = jnp.zeros(ys_ref.shape, U32)


def _grouped_ffn(xs, tile_expert, n_used, w1, b1, w2, b2, *, tm):
    n_rows = xs.shape[0]
    n_tiles = n_rows // tm
    row_blk = lambda i, te, nu: (jnp.minimum(i, nu[0] - 1), 0)
    exp_blk = lambda i, te, nu: (te[i], 0, 0)
    grid_spec = pltpu.PrefetchScalarGridSpec(
        num_scalar_prefetch=2,
        grid=(n_tiles,),
        in_specs=[
            pl.BlockSpec((tm, HALF), row_blk),
            pl.BlockSpec((1, D_MODEL, 2 * D_FF), exp_blk),
            pl.BlockSpec((1, 1, 2 * D_FF), exp_blk),
            pl.BlockSpec((1, D_FF, D_MODEL), exp_blk),
            pl.BlockSpec((1, 1, D_MODEL), exp_blk),
        ],
        out_specs=pl.BlockSpec((tm, HALF), lambda i, te, nu: (i, 0)),
        scratch_shapes=[pltpu.VMEM((D_MODEL, 2 * D_FF), BF16), pltpu.VMEM((D_FF, D_MODEL), BF16)],
    )
    return pl.pallas_call(
        _ffn_kernel,
        grid_spec=grid_spec,
        out_shape=jax.ShapeDtypeStruct((n_rows, HALF), U32),
        compiler_params=_cparams(1),
        name="moe_grouped_ffn",
    )(tile_expert, n_used, xs, w1, b1, w2, b2)


def _combine_kernel(seg_ref, x_ref, lpos_ref, tw_ref, g_ref, b_ref, ys_hbm, o_ref, rows_scr, sem,
                    *, tt, n_sorted):
    @pl.when(pl.program_id(0) == 0)
    def _():
        rows_scr[...] = jnp.zeros(rows_scr.shape, U32)

    def seg_copy(lo, src, size):
        return pltpu.make_async_copy(ys_hbm.at[pl.ds(src, size), :],
                                     rows_scr.at[pl.ds(lo, size), :], sem)

    _segment_starts(seg_ref, seg_copy)

    lp = lpos_ref[...]
    tw = tw_ref[...]
    c_iota = lax.broadcasted_iota(I32, (tt, n_sorted), 1)
    wmat = jnp.where(c_iota == lp[:, 0:1], tw[:, 0:1], 0.0)
    for k in range(1, TOP_K):
        wmat = jnp.where(c_iota == lp[:, k:k + 1], tw[:, k:k + 1], wmat)
    wmat = wmat.astype(BF16)

    _segment_waits(seg_ref, seg_copy, n_sorted=n_sorted)

    rows = rows_scr[...]
    r_lo = lax.bitcast_convert_type(rows << 16, F32).astype(BF16)
    r_hi = lax.bitcast_convert_type(rows & jnp.uint32(0xFFFF0000), F32).astype(BF16)
    y = jnp.concatenate([_dot(wmat, r_lo), _dot(wmat, r_hi)], axis=1)
    o_ref[...] = _layer_norm(DN_ALPHA * x_ref[...] + y, g_ref[...], b_ref[...])


def _combine(x2, ys, segs, lpos_tok, tw_tok, g, b, *, tt, n_sorted):
    n_tok = x2.shape[0]
    const2 = lambda i: (0, 0)
    return pl.pallas_call(
        functools.partial(_combine_kernel, tt=tt, n_sorted=n_sorted),
        grid=(n_tok // tt,),
        in_specs=[
            pl.BlockSpec((1, 1, SEG_WIDTH), lambda i: (i, 0, 0), memory_space=pltpu.SMEM),
            pl.BlockSpec((tt, D_MODEL), lambda i: (i, 0)),
            pl.BlockSpec((tt, TOP_K), lambda i: (i, 0)),
            pl.BlockSpec((tt, TOP_K), lambda i: (i, 0)),
            pl.BlockSpec((1, D_MODEL), const2),
            pl.BlockSpec((1, D_MODEL), const2),
            pl.BlockSpec(memory_space=pl.ANY),
        ],
        out_specs=pl.BlockSpec((tt, D_MODEL), lambda i: (i, 0)),
        out_shape=jax.ShapeDtypeStruct((n_tok, D_MODEL), F32),
        scratch_shapes=[pltpu.VMEM((n_sorted, HALF), U32), pltpu.SemaphoreType.DMA],
        compiler_params=_cparams(1),
        name="moe_combine",
    )(segs, x2, lpos_tok, tw_tok, g.reshape(1, -1), b.reshape(1, -1), ys)


def _moe_layer(x2, x2b, tw, lpos, meta, counts, w1, b1, w2, b2, g, b, layer, *, tm, tt):
    bsz, s, _ = x2.shape
    n_tok = bsz * s
    n_tok_tiles = n_tok // tt
    seg_pad = N_EXPERTS * (SEG_ALIGN - 1)
    n_sorted = -(-(TOP_K * tt + seg_pad) // MXU_DIM) * MXU_DIM
    n_tiles = -(-(n_tok * TOP_K + n_tok_tiles * seg_pad) // tm) + N_EXPERTS
    n_rows = n_tiles * tm

    tiles_e = (counts + tm - 1) // tm
    tile_end = jnp.cumsum(tiles_e)
    start_e = (tile_end - tiles_e) * tm
    n_used = tile_end[-1]
    last_tile_row = jnp.where(tiles_e > 0, (tile_end - 1) * tm, -1).astype(I32)
    n_used_arr = n_used.reshape(1).astype(I32)
    tile_ids = jnp.minimum(jnp.arange(n_tiles, dtype=I32), n_used - 1)
    tile_expert = jnp.sum(tile_ids[:, None] >= tile_end[None, :], axis=1).astype(I32)
    seg_len, seg_lo = meta[:, :, 0], meta[:, :, 1]
    seg_total = jnp.broadcast_to(seg_lo[:, -1:] + seg_len[:, -1:], (n_tok_tiles, N_EXPERTS))
    segs = jnp.concatenate([seg_len, seg_lo, meta[:, :, 2] + start_e[None, :], seg_total], axis=1)
    segs = segs.reshape(n_tok_tiles, 1, SEG_WIDTH).astype(I32)
    lpos_tiles = lpos.reshape(bsz, TOP_K, s // tt, tt).transpose(0, 2, 1, 3)
    lpos_tiles = lpos_tiles.reshape(n_tok_tiles, TOP_K, tt)

    xs = _dispatch(x2b.reshape(n_tok, D_MODEL), lpos_tiles, segs, last_tile_row, n_used_arr,
                   n_rows, tt=tt, tm=tm, n_sorted=n_sorted)
    ys = _grouped_ffn(xs, tile_expert + layer * N_EXPERTS, n_used_arr, w1, b1, w2, b2, tm=tm)
    lpos_tok = lpos.transpose(0, 2, 1).reshape(n_tok, TOP_K)
    tw_tok = tw.transpose(0, 2, 1).reshape(n_tok, TOP_K)
    out = _combine(x2.reshape(n_tok, D_MODEL), ys, segs, lpos_tok, tw_tok, g, b, tt=tt,
                   n_sorted=n_sorted)
    return out.reshape(bsz, s, D_MODEL)


def _pick_tile(n, want):
    t = min(n, want)
    while n % t:
        t //= 2
    return t


def kernel(x, mem, ev_w_in, ev_conv_a, ev_conv_b, ev_conv_b_bias, ev_w_rgate, ev_b_rgate, ev_w_igate, ev_b_igate, ev_lambda, ev_w_out, od_w_in, od_b_gates, od_norm_g, od_w_out, xa_wq, xa_wk, xa_wv, xa_wo, moe_w_router, moe_b_router, moe_w1, moe_b1, moe_w2, moe_b2, ln_g, ln_b):
    bsz, s, _ = x.shape
    ts_even = _pick_tile(s, 512)
    ts_odd = _pick_tile(s, 256)
    ts_attn = _pick_tile(s, 512)
    n_sub_attn = 2 if s % (2 * ts_attn) == 0 else 1
    tm = 512

    w1_all = moe_w1.reshape(DEPTH * N_EXPERTS, D_MODEL, 2 * D_FF)
    w2_all = moe_w2.reshape(DEPTH * N_EXPERTS, D_FF, D_MODEL)
    b1_all = moe_b1.reshape(DEPTH * N_EXPERTS, 1, 2 * D_FF)
    b2_all = moe_b2.reshape(DEPTH * N_EXPERTS, 1, D_MODEL)
    for layer in range(DEPTH):
        j = layer // 2
        if layer % 2 == 0:
            x = _even_layer(x, ev_w_in[j], ev_conv_a[j], ev_conv_b[j], ev_conv_b_bias[j],
                            ev_w_rgate[j], ev_b_rgate[j], ev_w_igate[j], ev_b_igate[j],
                            ev_lambda[j], ev_w_out[j], ln_g[layer, 0], ln_b[layer, 0], ts=ts_even)
        else:
            x = _odd_layer(x, od_w_in[j], od_b_gates[j], od_norm_g[j], od_w_out[j],
                           ln_g[layer, 0], ln_b[layer, 0], ts=ts_odd)
        k_mem, v_mem = _kv_proj(mem, xa_wk[layer], xa_wv[layer])
        x2, x2b, tw, lpos, meta, cnt = _attn_router(
            x, k_mem, v_mem, xa_wq[layer], xa_wo[layer], ln_g[layer, 1], ln_b[layer, 1],
            moe_w_router[layer], moe_b_router[layer], ts=ts_attn, n_sub=n_sub_attn)
        x = _moe_layer(x2, x2b, tw, lpos, meta, cnt[:, 0], w1_all, b1_all, w2_all, b2_all,
                       ln_g[layer, 2], ln_b[layer, 2], layer, tm=tm, tt=ts_attn)
    return x
```

```python
import functools

import jax
import jax.numpy as jnp
from jax import lax
from jax.experimental import pallas as pl
from jax.experimental.pallas import tpu as pltpu

F32 = jnp.float32
BF16 = jnp.bfloat16
U32 = jnp.uint32
I32 = jnp.int32

D_MODEL = 1024
DEPTH = 4
A_WIDTH = 512
B_WIDTH = 1024
B_HEADS = 8
B_HEAD_DIM = 128
LRU_C = 8.0
M_HEADS = 4
M_QK_DIM = 128
M_V_DIM = 256
M_QK = 512
M_V = 1024
M_CHUNK = 128
X_HEADS = 4
X_HEAD_DIM = 256
N_EXPERTS = 32
TOP_K = 4
D_FF = 1024
SWIGLU_LIMIT = 7.0
SWIGLU_ALPHA = 1.702
DN_ALPHA = (2 * DEPTH) ** 0.25
LN_EPS = 1e-5
RMS_EPS = 1e-6
HALF = D_MODEL // 2
SEG_ALIGN = 8
MXU_DIM = 256

VMEM_LIMIT_BYTES = 56 * 1024 * 1024


def _cparams(n_grid):
    return pltpu.CompilerParams(
        dimension_semantics=("arbitrary",) * n_grid, vmem_limit_bytes=VMEM_LIMIT_BYTES)


def _layer_norm(y, g, b):
    mu = jnp.mean(y, axis=-1, keepdims=True)
    yc = y - mu
    var = jnp.mean(yc * yc, axis=-1, keepdims=True)
    return yc * lax.rsqrt(var + LN_EPS) * g + b


def _sigmoid(x):
    return 1.0 / (1.0 + jnp.exp(-x))


def _softplus(x):
    return jnp.maximum(x, 0.0) + jnp.log1p(jnp.exp(-jnp.abs(x)))


def _gelu_tanh(x):
    return 0.5 * x * (1.0 + jnp.tanh(0.7978845608028654 * (x + 0.044715 * (x * x * x))))


def _split_bf16(x):
    hi = x.astype(BF16)
    lo = (x - hi.astype(F32)).astype(BF16)
    return hi, lo


def _dot(a, b):
    return jnp.dot(a, b, preferred_element_type=F32)


def _dot_nt(a, b):
    return lax.dot_general(a, b, (((1,), (1,)), ((), ())), preferred_element_type=F32)


def _dot_tn(a, b):
    return lax.dot_general(a, b, (((0,), (0,)), ((), ())), preferred_element_type=F32)


def _pack_rows(y):
    bits = lax.bitcast_convert_type(y.astype(BF16).astype(F32), U32)
    return (bits[:, :HALF] >> 16) | (bits[:, HALF:] & jnp.uint32(0xFFFF0000))


def _unpack_rows(w):
    lo = lax.bitcast_convert_type(w << 16, F32)
    hi = lax.bitcast_convert_type(w & jnp.uint32(0xFFFF0000), F32)
    return jnp.concatenate([lo, hi], axis=1)


def _even_kernel(x_ref, win_ref, ca_ref, cb_ref, cbb_ref, wr_ref, br_ref, wi_ref, bi_ref, lam_ref,
                 wout_ref, g_ref, b_ref, o_ref,
                 av_scr, bu_scr, a_scr, b_scr, h_scr, carry_scr, *, ts):
    @pl.when(pl.program_id(1) == 0)
    def _():
        av_scr[0:8, :] = jnp.zeros((8, A_WIDTH), F32)
        bu_scr[0:8, :] = jnp.zeros((8, B_WIDTH), F32)
        carry_scr[...] = jnp.zeros((8, B_WIDTH), F32)

    x = x_ref[0]
    z = _dot(x.astype(BF16), win_ref[...])
    a_b = z[:, 0:A_WIDTH]
    a_c = z[:, A_WIDTH:2 * A_WIDTH]
    a_x = z[:, 2 * A_WIDTH:3 * A_WIDTH]
    b_u = z[:, 3 * A_WIDTH:3 * A_WIDTH + B_WIDTH]
    b_g = z[:, 3 * A_WIDTH + B_WIDTH:]

    av_scr[8:8 + ts, :] = a_c * a_x
    ca = ca_ref[...]
    conv_a = (ca[2:3, :] * av_scr[8:8 + ts, :] + ca[1:2, :] * av_scr[7:7 + ts, :]
              + ca[0:1, :] * av_scr[6:6 + ts, :])
    y_a = a_b * conv_a
    av_scr[0:8, :] = av_scr[ts:ts + 8, :]

    bu_scr[8:8 + ts, :] = b_u
    cb = cb_ref[...]
    u = (cb[3:4, :] * bu_scr[8:8 + ts, :] + cb[2:3, :] * bu_scr[7:7 + ts, :]
         + cb[1:2, :] * bu_scr[6:6 + ts, :] + cb[0:1, :] * bu_scr[5:5 + ts, :] + cbb_ref[...])
    bu_scr[0:8, :] = bu_scr[ts:ts + 8, :]

    ub = u.astype(BF16)
    r_parts, i_parts = [], []
    for h in range(B_HEADS):
        uh = ub[:, h * B_HEAD_DIM:(h + 1) * B_HEAD_DIM]
        r_parts.append(_dot(uh, wr_ref[h]))
        i_parts.append(_dot(uh, wi_ref[h]))
    r = _sigmoid(jnp.concatenate(r_parts, axis=1) + br_ref[...])
    ig = _sigmoid(jnp.concatenate(i_parts, axis=1) + bi_ref[...])
    log_a = (-LRU_C) * r * _softplus(-lam_ref[...])
    a = jnp.exp(log_a)
    a_scr[...] = a
    b_scr[...] = jnp.sqrt(1.0 - a * a) * (ig * u)

    row8 = lax.broadcasted_iota(I32, (8, B_WIDTH), 0)

    def group(i, carry):
        r0 = pl.multiple_of(i * 8, 8)
        ga = a_scr[pl.ds(r0, 8), :]
        gb = b_scr[pl.ds(r0, 8), :]
        for d in (1, 2, 4):
            keep = row8 >= d
            a_sh = jnp.where(keep, pltpu.roll(ga, d, 0), 1.0)
            b_sh = jnp.where(keep, pltpu.roll(gb, d, 0), 0.0)
            gb = ga * b_sh + gb
            ga = ga * a_sh
        hg = gb + ga * carry
        h_scr[pl.ds(r0, 8), :] = hg
        return jnp.broadcast_to(hg[7:8, :], (8, B_WIDTH))

    carry_scr[...] = lax.fori_loop(0, ts // 8, group, carry_scr[...], unroll=4)

    y_b = _gelu_tanh(b_g) * h_scr[...]
    mix = (_dot(y_a.astype(BF16), wout_ref[0:A_WIDTH, :])
           + _dot(y_b.astype(BF16), wout_ref[A_WIDTH:, :]))
    o_ref[0] = _layer_norm(DN_ALPHA * x + mix, g_ref[...], b_ref[...])


def _even_layer(x, w_in, conv_a, conv_b, conv_b_bias, w_r, b_r, w_i, b_i, lam, w_out, g, b, *, ts):
    bsz, s, _ = x.shape
    ev_in = 3 * A_WIDTH + 2 * B_WIDTH
    const2 = lambda bi, si: (0, 0)
    const3 = lambda bi, si: (0, 0, 0)
    return pl.pallas_call(
        functools.partial(_even_kernel, ts=ts),
        grid=(bsz, s // ts),
        in_specs=[
            pl.BlockSpec((1, ts, D_MODEL), lambda bi, si: (bi, si, 0)),
            pl.BlockSpec((D_MODEL, ev_in), const2),
            pl.BlockSpec((3, A_WIDTH), const2),
            pl.BlockSpec((4, B_WIDTH), const2),
            pl.BlockSpec((1, B_WIDTH), const2),
            pl.BlockSpec((B_HEADS, B_HEAD_DIM, B_HEAD_DIM), const3),
            pl.BlockSpec((1, B_WIDTH), const2),
            pl.BlockSpec((B_HEADS, B_HEAD_DIM, B_HEAD_DIM), const3),
            pl.BlockSpec((1, B_WIDTH), const2),
            pl.BlockSpec((1, B_WIDTH), const2),
            pl.BlockSpec((A_WIDTH + B_WIDTH, D_MODEL), const2),
            pl.BlockSpec((1, D_MODEL), const2),
            pl.BlockSpec((1, D_MODEL), const2),
        ],
        out_specs=pl.BlockSpec((1, ts, D_MODEL), lambda bi, si: (bi, si, 0)),
        out_shape=jax.ShapeDtypeStruct(x.shape, F32),
        scratch_shapes=[
            pltpu.VMEM((ts + 8, A_WIDTH), F32),
            pltpu.VMEM((ts + 8, B_WIDTH), F32),
            pltpu.VMEM((ts, B_WIDTH), F32),
            pltpu.VMEM((ts, B_WIDTH), F32),
            pltpu.VMEM((ts, B_WIDTH), F32),
            pltpu.VMEM((8, B_WIDTH), F32),
        ],
        compiler_params=_cparams(2),
        name="even_mixer",
    )(x, w_in.astype(BF16), conv_a, conv_b, conv_b_bias.reshape(1, -1), w_r.astype(BF16),
      b_r.reshape(1, -1), w_i.astype(BF16), b_i.reshape(1, -1), lam.reshape(1, -1),
      w_out.astype(BF16), g.reshape(1, -1), b.reshape(1, -1))


GATE_PAD = 128
V_EXT = M_V_DIM + 128


def _odd_kernel(x_ref, win_ref, wg_ref, bgc_ref, ng_ref, wout_ref, g_ref, b_ref,
                o_ref, c_scr, m_scr, h_scr, *, ts):
    L = M_CHUNK

    @pl.when(pl.program_id(1) == 0)
    def _():
        c_scr[...] = jnp.zeros(c_scr.shape, F32)
        m_scr[...] = jnp.zeros(m_scr.shape, F32)

    x = x_ref[0]
    x_hi, x_lo = _split_bf16(x)
    z = _dot(x_hi, win_ref[...])
    q_all = (z[:, 0:M_QK] * (M_QK_DIM ** -0.5)).astype(BF16)
    k_all = z[:, M_QK:2 * M_QK].astype(BF16)
    v_all = z[:, 2 * M_QK:2 * M_QK + M_V].astype(BF16)
    o_all = z[:, 2 * M_QK + M_V:]

    wg_hi, wg_lo = _split_bf16(wg_ref[...])
    gates_c = _dot(x_hi, wg_hi) + _dot(x_lo, wg_hi) + _dot(x_hi, wg_lo) + bgc_ref[...]
    logf_c = -_softplus(-gates_c)

    rows = lax.broadcasted_iota(I32, (L, L), 0)
    cols = lax.broadcasted_iota(I32, (L, L), 1)
    causal = rows >= cols
    tril = jnp.where(causal, 1.0, 0.0).astype(BF16)
    lane0 = jnp.where(lax.broadcasted_iota(I32, (L, 128), 1) == 0, 1.0, 0.0).astype(BF16)

    for c in range(ts // L):
        sl = slice(c * L, (c + 1) * L)
        fc_hi, fc_lo = _split_bf16(logf_c[sl, :])
        fc_lo2 = (logf_c[sl, :] - fc_hi.astype(F32) - fc_lo.astype(F32)).astype(BF16)
        bcum_c = _dot(tril, fc_hi) + _dot(tril, fc_lo) + _dot(tril, fc_lo2)
        i_minus_b = gates_c[sl, :] - pltpu.roll(bcum_c, GATE_PAD - M_HEADS, 1)
        i_minus_b_t = i_minus_b.T
        for h in range(M_HEADS):
            m_st = m_scr[h][0:1, 0:1]
            b_col = bcum_c[:, M_HEADS + h:M_HEADS + h + 1]
            i_col = gates_c[sl, h:h + 1]
            b_end = b_col[L - 1:L, :]

            d = jnp.where(causal, b_col + i_minus_b_t[h:h + 1, :], -jnp.inf)
            inter = b_col + m_st
            m_t = jnp.maximum(inter, jnp.max(d, axis=1, keepdims=True))
            p = jnp.exp(d - m_t)
            w_inter = jnp.exp(inter - m_t)

            qh = q_all[sl, h * M_QK_DIM:(h + 1) * M_QK_DIM]
            kh = k_all[sl, h * M_QK_DIM:(h + 1) * M_QK_DIM]
            v_ext = jnp.concatenate([v_all[sl, h * M_V_DIM:(h + 1) * M_V_DIM], lane0], axis=1)
            qk = (_dot_nt(qh, kh) * p).astype(BF16)
            c_ext = c_scr[h]
            nd = _dot(qk, v_ext) + w_inter * _dot(qh, c_ext.astype(BF16))
            den = nd[:, M_V_DIM:M_V_DIM + 1]
            hh = nd[:, 0:M_V_DIM] / jnp.maximum(jnp.abs(den), jnp.exp(-m_t))

            g_col = b_end - b_col + i_col
            m_new = jnp.maximum(b_end + m_st, jnp.max(g_col, axis=0, keepdims=True))
            wg_col = jnp.exp(g_col - m_new)
            decay = jnp.exp(b_end + m_st - m_new)
            kv = _dot_tn(kh, (wg_col * v_ext.astype(F32)).astype(BF16))
            c_scr[h] = decay * c_ext + kv
            m_scr[h] = jnp.broadcast_to(m_new, (8, 128))

            hn = hh * lax.rsqrt(jnp.mean(hh * hh, axis=-1, keepdims=True) + RMS_EPS)
            h_scr[sl, h * M_V_DIM:(h + 1) * M_V_DIM] = hn

    gated = _sigmoid(o_all) * (h_scr[...] * ng_ref[...])
    mix = _dot(gated.astype(BF16), wout_ref[...])
    o_ref[0] = _layer_norm(DN_ALPHA * x + mix, g_ref[...], b_ref[...])


def _odd_layer(x, w_in, b_gates, norm_g, w_out, g, b, *, ts):
    bsz, s, _ = x.shape
    n_main = 2 * M_QK + 2 * M_V
    w_main = w_in[:, :n_main].astype(BF16)
    w_gate = w_in[:, n_main:]
    wg_pad = jnp.pad(w_gate, ((0, 0), (0, GATE_PAD - 2 * M_HEADS)))
    bg_col = jnp.pad(b_gates, (0, GATE_PAD - 2 * M_HEADS)).reshape(1, GATE_PAD)
    const2 = lambda bi, si: (0, 0)
    return pl.pallas_call(
        functools.partial(_odd_kernel, ts=ts),
        grid=(bsz, s // ts),
        in_specs=[
            pl.BlockSpec((1, ts, D_MODEL), lambda bi, si: (bi, si, 0)),
            pl.BlockSpec((D_MODEL, n_main), const2),
            pl.BlockSpec((D_MODEL, GATE_PAD), const2),
            pl.BlockSpec((1, GATE_PAD), const2),
            pl.BlockSpec((1, M_V), const2),
            pl.BlockSpec((M_V, D_MODEL), const2),
            pl.BlockSpec((1, D_MODEL), const2),
            pl.BlockSpec((1, D_MODEL), const2),
        ],
        out_specs=pl.BlockSpec((1, ts, D_MODEL), lambda bi, si: (bi, si, 0)),
        out_shape=jax.ShapeDtypeStruct(x.shape, F32),
        scratch_shapes=[
            pltpu.VMEM((M_HEADS, M_QK_DIM, V_EXT), F32),
            pltpu.VMEM((M_HEADS, 8, 128), F32),
            pltpu.VMEM((ts, M_V), F32),
        ],
        compiler_params=_cparams(2),
        name="odd_mixer",
    )(x, w_main, wg_pad, bg_col, norm_g.reshape(1, -1), w_out.astype(BF16),
      g.reshape(1, -1), b.reshape(1, -1))


def _kv_kernel(mem_ref, wk_ref, wv_ref, k_ref, v_ref):
    m = mem_ref[0].astype(BF16)
    k_ref[0] = _dot(m, wk_ref[...]).astype(BF16)
    v_ref[0] = _dot(m, wv_ref[...]).astype(BF16)


def _kv_proj(mem, wk, wv):
    bsz, n_mem, _ = mem.shape
    const2 = lambda bi: (0, 0)
    blk = pl.BlockSpec((1, n_mem, D_MODEL), lambda bi: (bi, 0, 0))
    return pl.pallas_call(
        _kv_kernel,
        grid=(bsz,),
        in_specs=[blk, pl.BlockSpec((D_MODEL, D_MODEL), const2),
                  pl.BlockSpec((D_MODEL, D_MODEL), const2)],
        out_specs=[blk, blk],
        out_shape=[jax.ShapeDtypeStruct(mem.shape, BF16)] * 2,
        compiler_params=_cparams(1),
        name="memory_kv",
    )(mem, wk.astype(BF16), wv.astype(BF16))


def _attn_tile(x, k, v, wq_ref, wo_ref, g_ref, b_ref, wrt_ref, brt_ref, taken):
    ts = x.shape[0]
    q = _dot(x.astype(BF16), wq_ref[...]).astype(BF16)
    heads = []
    for h in range(X_HEADS):
        hs = slice(h * X_HEAD_DIM, (h + 1) * X_HEAD_DIM)
        sc = _dot_nt(q[:, hs], k[:, hs]) * (X_HEAD_DIM ** -0.5)
        e = jnp.exp(sc - jnp.max(sc, axis=-1, keepdims=True))
        p = e / jnp.sum(e, axis=-1, keepdims=True)
        heads.append(_dot(p.astype(BF16), v[:, hs]))
    att = _dot(jnp.concatenate(heads, axis=1).astype(BF16), wo_ref[...])
    x2 = _layer_norm(DN_ALPHA * x + att, g_ref[...], b_ref[...])

    x_hi, x_lo = _split_bf16(x2)
    w_hi, w_lo = _split_bf16(wrt_ref[...])
    logits = _dot_nt(w_hi, x_hi) + _dot_nt(w_hi, x_lo) + _dot_nt(w_lo, x_hi) + brt_ref[...]
    e_iota = lax.broadcasted_iota(I32, (N_EXPERTS, ts), 0)
    onehots, vals = [], []
    for _ in range(TOP_K):
        mx = jnp.max(logits, axis=0, keepdims=True)
        sel = jnp.min(jnp.where(logits == mx, e_iota, N_EXPERTS), axis=0, keepdims=True)
        oh = e_iota == sel
        onehots.append(oh)
        vals.append(mx)
        logits = jnp.where(oh, -jnp.inf, logits)
    exps = [jnp.exp(val - vals[0]) for val in vals]
    tot = exps[0] + exps[1] + exps[2] + exps[3]
    tw = jnp.concatenate([ex / tot for ex in exps], axis=0)

    member = jnp.where(onehots[0] | onehots[1] | onehots[2] | onehots[3], 1.0, 0.0)
    t_r = lax.broadcasted_iota(I32, (ts, ts), 0)
    t_c = lax.broadcasted_iota(I32, (ts, ts), 1)
    before = jnp.where(t_r < t_c, 1.0, 0.0).astype(BF16)
    pos = _dot(member.astype(BF16), before)
    n_e = jnp.sum(member, axis=1, keepdims=True)
    q_e = jnp.floor((n_e + (SEG_ALIGN - 1.0)) * (1.0 / SEG_ALIGN))
    e_r = lax.broadcasted_iota(I32, (N_EXPERTS, N_EXPERTS), 0)
    e_c = lax.broadcasted_iota(I32, (N_EXPERTS, N_EXPERTS), 1)
    lower = jnp.where(e_c < e_r, 1.0, 0.0).astype(BF16)
    q_wide = jnp.broadcast_to(q_e, (N_EXPERTS, 128)).astype(BF16)
    lo_e = SEG_ALIGN * _dot(lower, q_wide)[:, 0:1]
    p_e = SEG_ALIGN * q_e
    base = lo_e + pos
    lpos = [jnp.sum(jnp.where(oh, base, 0.0), axis=0, keepdims=True) for oh in onehots]
    lpos = jnp.concatenate(lpos, axis=0).astype(I32)
    lane = lax.broadcasted_iota(I32, (N_EXPERTS, 128), 1)
    meta = jnp.where(lane == 0, p_e, jnp.where(lane == 1, lo_e, jnp.where(lane == 2, taken, 0.0)))
    return x2, x_hi, tw, lpos, meta.astype(I32), taken + p_e


def _attn_kernel(x_ref, k_ref, v_ref, wq_ref, wo_ref, g_ref, b_ref, wrt_ref, brt_ref,
                 x2_ref, x2b_ref, tw_ref, lpos_ref, meta_ref, cnt_ref, cnt_scr, *, ts, n_sub):
    @pl.when((pl.program_id(0) == 0) & (pl.program_id(1) == 0))
    def _():
        cnt_scr[...] = jnp.zeros(cnt_scr.shape, F32)

    taken = cnt_scr[...]
    for u in range(n_sub):
        rs = slice(u * ts, (u + 1) * ts)
        x2, x_hi, tw, lpos, meta, taken = _attn_tile(
            x_ref[0, rs, :], k_ref[0], v_ref[0], wq_ref, wo_ref, g_ref, b_ref, wrt_ref, brt_ref,
            taken)
        x2_ref[0, rs, :] = x2
        x2b_ref[0, rs, :] = x_hi
        tw_ref[0, :, rs] = tw
        lpos_ref[0, :, rs] = lpos
        meta_ref[u] = meta
    cnt_scr[...] = taken
    cnt_ref[...] = taken.astype(I32)


def _attn_router(x, k_mem, v_mem, wq, wo, g, b, w_router, b_router, *, ts, n_sub):
    bsz, s, _ = x.shape
    n_mem = k_mem.shape[1]
    tg = ts * n_sub
    const2 = lambda bi, si: (0, 0)
    tok = lambda bi, si: (bi, si, 0)
    lanes = lambda bi, si: (bi, 0, si)
    return pl.pallas_call(
        functools.partial(_attn_kernel, ts=ts, n_sub=n_sub),
        grid=(bsz, s // tg),
        in_specs=[
            pl.BlockSpec((1, tg, D_MODEL), tok),
            pl.BlockSpec((1, n_mem, D_MODEL), lambda bi, si: (bi, 0, 0)),
            pl.BlockSpec((1, n_mem, D_MODEL), lambda bi, si: (bi, 0, 0)),
            pl.BlockSpec((D_MODEL, D_MODEL), const2),
            pl.BlockSpec((D_MODEL, D_MODEL), const2),
            pl.BlockSpec((1, D_MODEL), const2),
            pl.BlockSpec((1, D_MODEL), const2),
            pl.BlockSpec((N_EXPERTS, D_MODEL), const2),
            pl.BlockSpec((N_EXPERTS, 1), const2),
        ],
        out_specs=[
            pl.BlockSpec((1, tg, D_MODEL), tok),
            pl.BlockSpec((1, tg, D_MODEL), tok),
            pl.BlockSpec((1, TOP_K, tg), lanes),
            pl.BlockSpec((1, TOP_K, tg), lanes),
            pl.BlockSpec((n_sub, N_EXPERTS, 128), lambda bi, si: (bi * (s // tg) + si, 0, 0)),
            pl.BlockSpec((N_EXPERTS, 128), const2),
        ],
        out_shape=[
            jax.ShapeDtypeStruct((bsz, s, D_MODEL), F32),
            jax.ShapeDtypeStruct((bsz, s, D_MODEL), BF16),
            jax.ShapeDtypeStruct((bsz, TOP_K, s), F32),
            jax.ShapeDtypeStruct((bsz, TOP_K, s), I32),
            jax.ShapeDtypeStruct((bsz * (s // ts), N_EXPERTS, 128), I32),
            jax.ShapeDtypeStruct((N_EXPERTS, 128), I32),
        ],
        scratch_shapes=[pltpu.VMEM((N_EXPERTS, 128), F32)],
        compiler_params=_cparams(2),
        name="memory_attention_router",
    )(x, k_mem, v_mem, wq.astype(BF16), wo.astype(BF16), g.reshape(1, -1), b.reshape(1, -1),
      w_router.T, b_router.reshape(-1, 1))


WMAT_BLOCK = 16
SEG_CHUNK = 128
SEG_WIDTH = 4 * N_EXPERTS


def _segment_starts(seg_ref, make_copy):
    chunk_bit = SEG_CHUNK.bit_length() - 1

    def per_expert(e, carry):
        n = seg_ref[0, 0, e]
        lo = seg_ref[0, 0, N_EXPERTS + e]
        dst = seg_ref[0, 0, 2 * N_EXPERTS + e]

        def chunk(j, c):
            off = pl.multiple_of(j * SEG_CHUNK, SEG_CHUNK)
            make_copy(pl.multiple_of(lo + off, SEG_ALIGN), pl.multiple_of(dst + off, SEG_ALIGN),
                      SEG_CHUNK).start()
            return c

        lax.fori_loop(0, n >> chunk_bit, chunk, 0)
        for bit in range(chunk_bit - 1, SEG_ALIGN.bit_length() - 2, -1):
            size = 1 << bit
            done = (n >> (bit + 1)) << (bit + 1)

            @pl.when((n & size) != 0)
            def _():
                make_copy(pl.multiple_of(lo + done, SEG_ALIGN),
                          pl.multiple_of(dst + done, SEG_ALIGN), size).start()
        return carry

    lax.fori_loop(0, N_EXPERTS, per_expert, 0)


def _segment_waits(total, make_copy, *, n_sorted):
    for bit in range(n_sorted.bit_length() - 1, SEG_ALIGN.bit_length() - 2, -1):
        size = 1 << bit

        @pl.when((total & size) != 0)
        def _():
            make_copy(0, 0, size).wait()


def _dispatch_kernel(last_ref, nused_ref, total_ref, seg_ref, lpos_ref, x_ref, xs_hbm, sort_scr,
                     zero_scr, sem, zsem, *, tt, tm, n_tiles, n_sorted):
    @pl.when(pl.program_id(0) == 0)
    def _():
        zero_scr[...] = jnp.zeros(zero_scr.shape, U32)

        def zero_copy(row):
            return pltpu.make_async_copy(
                zero_scr, xs_hbm.at[pl.ds(pl.multiple_of(row, tm), tm), :], zsem)

        for e in range(N_EXPERTS):
            @pl.when(last_ref[e] >= 0)
            def _():
                zero_copy(last_ref[e]).start()

        def start_tail(i, carry):
            zero_copy(i * tm).start()
            return carry

        lax.fori_loop(nused_ref[0], n_tiles, start_tail, 0)

        for e in range(N_EXPERTS):
            @pl.when(last_ref[e] >= 0)
            def _():
                zero_copy(last_ref[e]).wait()

        def wait_tail(i, carry):
            zero_copy(i * tm).wait()
            return carry

        lax.fori_loop(nused_ref[0], n_tiles, wait_tail, 0)

    lp = lpos_ref[0]
    r_iota = lax.broadcasted_iota(I32, (n_sorted, tt), 0)
    perm = jnp.where(r_iota == lp[0:1, :], 1.0, 0.0)
    for k in range(1, TOP_K):
        perm = jnp.where(r_iota == lp[k:k + 1, :], 1.0, perm)
    perm = perm.astype(BF16)
    lo_bits = lax.bitcast_convert_type(_dot(perm, x_ref[:, 0:HALF]), U32)
    hi_bits = lax.bitcast_convert_type(_dot(perm, x_ref[:, HALF:]), U32)
    i = pl.program_id(0)
    buf = lax.rem(i, 2)
    sort_scr[buf] = (lo_bits >> 16) | (hi_bits & jnp.uint32(0xFFFF0000))

    def seg_copy_on(slot):
        def seg_copy(lo, dst, size):
            return pltpu.make_async_copy(sort_scr.at[slot, pl.ds(lo, size), :],
                                         xs_hbm.at[pl.ds(dst, size), :], sem.at[slot])
        return seg_copy

    _segment_starts(seg_ref, seg_copy_on(buf))

    @pl.when(i > 0)
    def _():
        _segment_waits(total_ref[i - 1], seg_copy_on(1 - buf), n_sorted=n_sorted)

    @pl.when(i == pl.num_programs(0) - 1)
    def _():
        _segment_waits(total_ref[i], seg_copy_on(buf), n_sorted=n_sorted)


def _dispatch(x2b, lpos_tiles, segs, last_tile_row, n_used, seg_totals, n_rows, *, tt, tm,
              n_sorted):
    n_tok = x2b.shape[0]
    grid_spec = pltpu.PrefetchScalarGridSpec(
        num_scalar_prefetch=3,
        grid=(n_tok // tt,),
        in_specs=[
            pl.BlockSpec((1, 1, SEG_WIDTH), lambda i, la, nu, to: (i, 0, 0),
                         memory_space=pltpu.SMEM),
            pl.BlockSpec((1, TOP_K, tt), lambda i, la, nu, to: (i, 0, 0)),
            pl.BlockSpec((tt, D_MODEL), lambda i, la, nu, to: (i, 0)),
        ],
        out_specs=pl.BlockSpec(memory_space=pl.ANY),
        scratch_shapes=[pltpu.VMEM((2, n_sorted, HALF), U32), pltpu.VMEM((tm, HALF), U32),
                        pltpu.SemaphoreType.DMA((2,)), pltpu.SemaphoreType.DMA],
    )
    return pl.pallas_call(
        functools.partial(_dispatch_kernel, tt=tt, tm=tm, n_tiles=n_rows // tm,
                          n_sorted=n_sorted),
        grid_spec=grid_spec,
        out_shape=jax.ShapeDtypeStruct((n_rows, HALF), U32),
        compiler_params=pltpu.CompilerParams(
            dimension_semantics=("arbitrary",), has_side_effects=True,
            vmem_limit_bytes=VMEM_LIMIT_BYTES),
        name="moe_dispatch",
    )(last_tile_row, n_used, seg_totals, segs, lpos_tiles, x2b)


def _ffn_kernel(te_ref, nused_ref, xs_ref, w1_ref, b1_ref, w2_ref, b2_ref, ys_ref, w1_scr, w2_scr):
    i = pl.program_id(0)

    @pl.when((i == 0) | (te_ref[i] != te_ref[jnp.maximum(i - 1, 0)]))
    def _():
        w1_scr[...] = w1_ref[0].astype(BF16)
        w2_scr[...] = w2_ref[0].astype(BF16)

    @pl.when(i < nused_ref[0])
    def _():
        xt = _unpack_rows(xs_ref[...]).astype(BF16)
        gate = _dot(xt, w1_scr[:, 0:D_FF]) + b1_ref[0, :, 0:D_FF]
        up = _dot(xt, w1_scr[:, D_FF:]) + b1_ref[0, :, D_FF:]
        gate = jnp.minimum(gate, SWIGLU_LIMIT)
        up = jnp.clip(up, -SWIGLU_LIMIT, SWIGLU_LIMIT)
        hid = (up + 1.0) * gate * _sigmoid(SWIGLU_ALPHA * gate)
        ys_ref[...] = _pack_rows(_dot(hid.astype(BF16), w2_scr[...]) + b2_ref[0])

    @pl.when(i >= nused_ref[0])
    def _():
        ys_ref[...] = jnp.zeros(ys_ref.shape, U32)


def _grouped_ffn(xs, tile_expert, n_used, w1, b1, w2, b2, *, tm):
    n_rows = xs.shape[0]
    n_tiles = n_rows // tm
    row_blk = lambda i, te, nu: (jnp.minimum(i, nu[0] - 1), 0)
    exp_blk = lambda i, te, nu: (te[i], 0, 0)
    grid_spec = pltpu.PrefetchScalarGridSpec(
        num_scalar_prefetch=2,
        grid=(n_tiles,),
        in_specs=[
            pl.BlockSpec((tm, HALF), row_blk),
            pl.BlockSpec((1, D_MODEL, 2 * D_FF), exp_blk),
            pl.BlockSpec((1, 1, 2 * D_FF), exp_blk),
            pl.BlockSpec((1, D_FF, D_MODEL), exp_blk),
            pl.BlockSpec((1, 1, D_MODEL), exp_blk),
        ],
        out_specs=pl.BlockSpec((tm, HALF), lambda i, te, nu: (i, 0)),
        scratch_shapes=[pltpu.VMEM((D_MODEL, 2 * D_FF), BF16), pltpu.VMEM((D_FF, D_MODEL), BF16)],
    )
    return pl.pallas_call(
        _ffn_kernel,
        grid_spec=grid_spec,
        out_shape=jax.ShapeDtypeStruct((n_rows, HALF), U32),
        compiler_params=_cparams(1),
        name="moe_grouped_ffn",
    )(tile_expert, n_used, xs, w1, b1, w2, b2)


def _combine_kernel(seg_ref, segn_ref, x_ref, lpos_ref, tw_ref, g_ref, b_ref, ys_hbm, o_ref,
                    rows_scr, sem, *, tt, n_sorted):
    i = pl.program_id(0)
    buf = lax.rem(i, 2)

    def seg_copy_on(slot):
        def seg_copy(lo, src, size):
            return pltpu.make_async_copy(ys_hbm.at[pl.ds(src, size), :],
                                         rows_scr.at[slot, pl.ds(lo, size), :], sem.at[slot])
        return seg_copy

    @pl.when(i == 0)
    def _():
        rows_scr[...] = jnp.zeros(rows_scr.shape, U32)
        _segment_starts(seg_ref, seg_copy_on(0))

    @pl.when(i + 1 < pl.num_programs(0))
    def _():
        _segment_starts(segn_ref, seg_copy_on(1 - buf))

    lp = lpos_ref[...]
    tw = tw_ref[...]
    c_iota = lax.broadcasted_iota(I32, (WMAT_BLOCK, n_sorted), 1)
    blocks = []
    for t0 in range(0, tt, WMAT_BLOCK):
        lp0 = lp[t0:t0 + WMAT_BLOCK, :]
        tw0 = tw[t0:t0 + WMAT_BLOCK, :]
        blk = jnp.where(c_iota == lp0[:, 0:1], tw0[:, 0:1], 0.0)
        for k in range(1, TOP_K):
            blk = jnp.where(c_iota == lp0[:, k:k + 1], tw0[:, k:k + 1], blk)
        blocks.append(blk.astype(BF16))
    wmat = jnp.concatenate(blocks, axis=0)

    _segment_waits(seg_ref[0, 0, 3 * N_EXPERTS], seg_copy_on(buf), n_sorted=n_sorted)

    rows = rows_scr[buf]
    r_lo = lax.bitcast_convert_type(rows << 16, F32).astype(BF16)
    r_hi = lax.bitcast_convert_type(rows & jnp.uint32(0xFFFF0000), F32).astype(BF16)
    y = jnp.concatenate([_dot(wmat, r_lo), _dot(wmat, r_hi)], axis=1)
    o_ref[...] = _layer_norm(DN_ALPHA * x_ref[...] + y, g_ref[...], b_ref[...])


def _combine(x2, ys, segs, lpos_tok, tw_tok, g, b, *, tt, n_sorted):
    n_tok = x2.shape[0]
    const2 = lambda i: (0, 0)
    return pl.pallas_call(
        functools.partial(_combine_kernel, tt=tt, n_sorted=n_sorted),
        grid=(n_tok // tt,),
        in_specs=[
            pl.BlockSpec((1, 1, SEG_WIDTH), lambda i: (i, 0, 0), memory_space=pltpu.SMEM),
            pl.BlockSpec((1, 1, SEG_WIDTH), lambda i: (jnp.minimum(i + 1, n_tok // tt - 1), 0, 0),
                         memory_space=pltpu.SMEM),
            pl.BlockSpec((tt, D_MODEL), lambda i: (i, 0)),
            pl.BlockSpec((tt, TOP_K), lambda i: (i, 0)),
            pl.BlockSpec((tt, TOP_K), lambda i: (i, 0)),
            pl.BlockSpec((1, D_MODEL), const2),
            pl.BlockSpec((1, D_MODEL), const2),
            pl.BlockSpec(memory_space=pl.ANY),
        ],
        out_specs=pl.BlockSpec((tt, D_MODEL), lambda i: (i, 0)),
        out_shape=jax.ShapeDtypeStruct((n_tok, D_MODEL), F32),
        scratch_shapes=[pltpu.VMEM((2, n_sorted, HALF), U32), pltpu.SemaphoreType.DMA((2,))],
        compiler_params=_cparams(1),
        name="moe_combine",
    )(segs, segs, x2, lpos_tok, tw_tok, g.reshape(1, -1), b.reshape(1, -1), ys)


def _moe_layer(x2, x2b, tw, lpos, meta, counts, w1, b1, w2, b2, g, b, layer, *, tm, tt):
    bsz, s, _ = x2.shape
    n_tok = bsz * s
    n_tok_tiles = n_tok // tt
    seg_pad = N_EXPERTS * (SEG_ALIGN - 1)
    n_sorted = -(-(TOP_K * tt + seg_pad) // MXU_DIM) * MXU_DIM
    n_tiles = -(-(n_tok * TOP_K + n_tok_tiles * seg_pad) // tm) + N_EXPERTS
    n_rows = n_tiles * tm

    tiles_e = (counts + tm - 1) // tm
    tile_end = jnp.cumsum(tiles_e)
    start_e = (tile_end - tiles_e) * tm
    n_used = tile_end[-1]
    last_tile_row = jnp.where(tiles_e > 0, (tile_end - 1) * tm, -1).astype(I32)
    n_used_arr = n_used.reshape(1).astype(I32)
    tile_ids = jnp.minimum(jnp.arange(n_tiles, dtype=I32), n_used - 1)
    tile_expert = jnp.sum(tile_ids[:, None] >= tile_end[None, :], axis=1).astype(I32)
    seg_len, seg_lo = meta[:, :, 0], meta[:, :, 1]
    seg_total = jnp.broadcast_to(seg_lo[:, -1:] + seg_len[:, -1:], (n_tok_tiles, N_EXPERTS))
    segs = jnp.concatenate([seg_len, seg_lo, meta[:, :, 2] + start_e[None, :], seg_total], axis=1)
    segs = segs.reshape(n_tok_tiles, 1, SEG_WIDTH).astype(I32)
    lpos_tiles = lpos.reshape(bsz, TOP_K, s // tt, tt).transpose(0, 2, 1, 3)
    lpos_tiles = lpos_tiles.reshape(n_tok_tiles, TOP_K, tt)

    xs = _dispatch(x2b.reshape(n_tok, D_MODEL), lpos_tiles, segs, last_tile_row, n_used_arr,
                   seg_total[:, 0].astype(I32), n_rows, tt=tt, tm=tm, n_sorted=n_sorted)
    ys = _grouped_ffn(xs, tile_expert + layer * N_EXPERTS, n_used_arr, w1, b1, w2, b2, tm=tm)
    lpos_tok = lpos.transpose(0, 2, 1).reshape(n_tok, TOP_K)
    tw_tok = tw.transpose(0, 2, 1).reshape(n_tok, TOP_K)
    out = _combine(x2.reshape(n_tok, D_MODEL), ys, segs, lpos_tok, tw_tok, g, b, tt=tt,
                   n_sorted=n_sorted)
    return out.reshape(bsz, s, D_MODEL)


def _pick_tile(n, want):
    t = min(n, want)
    while n % t:
        t //= 2
    return t


def kernel(x, mem, ev_w_in, ev_conv_a, ev_conv_b, ev_conv_b_bias, ev_w_rgate, ev_b_rgate, ev_w_igate, ev_b_igate, ev_lambda, ev_w_out, od_w_in, od_b_gates, od_norm_g, od_w_out, xa_wq, xa_wk, xa_wv, xa_wo, moe_w_router, moe_b_router, moe_w1, moe_b1, moe_w2, moe_b2, ln_g, ln_b):
    bsz, s, _ = x.shape
    ts_even = _pick_tile(s, 512)
    ts_odd = _pick_tile(s, 256)
    ts_attn = _pick_tile(s, 512)
    n_sub_attn = 2 if s % (2 * ts_attn) == 0 else 1
    tm = 512

    w1_all = moe_w1.reshape(DEPTH * N_EXPERTS, D_MODEL, 2 * D_FF)
    w2_all = moe_w2.reshape(DEPTH * N_EXPERTS, D_FF, D_MODEL)
    b1_all = moe_b1.reshape(DEPTH * N_EXPERTS, 1, 2 * D_FF)
    b2_all = moe_b2.reshape(DEPTH * N_EXPERTS, 1, D_MODEL)
    for layer in range(DEPTH):
        j = layer // 2
        if layer % 2 == 0:
            x = _even_layer(x, ev_w_in[j], ev_conv_a[j], ev_conv_b[j], ev_conv_b_bias[j],
                            ev_w_rgate[j], ev_b_rgate[j], ev_w_igate[j], ev_b_igate[j],
                            ev_lambda[j], ev_w_out[j], ln_g[layer, 0], ln_b[layer, 0], ts=ts_even)
        else:
            x = _odd_layer(x, od_w_in[j], od_b_gates[j], od_norm_g[j], od_w_out[j],
                           ln_g[layer, 0], ln_b[layer, 0], ts=ts_odd)
        k_mem, v_mem = _kv_proj(mem, xa_wk[layer], xa_wv[layer])
        x2, x2b, tw, lpos, meta, cnt = _attn_router(
            x, k_mem, v_mem, xa_wq[layer], xa_wo[layer], ln_g[layer, 1], ln_b[layer, 1],
            moe_w_router[layer], moe_b_router[layer], ts=ts_attn, n_sub=n_sub_attn)
        x = _moe_layer(x2, x2b, tw, lpos, meta, cnt[:, 0], w1_all, b1_all, w2_all, b2_all,
                       ln_g[layer, 2], ln_b[layer, 2], layer, tm=tm, tt=ts_attn)
    return x
```

```python
import functools

import jax
import jax.numpy as jnp
from jax import lax
from jax.experimental import pallas as pl
from jax.experimental.pallas import tpu as pltpu

F32 = jnp.float32
BF16 = jnp.bfloat16
U32 = jnp.uint32
I32 = jnp.int32

D_MODEL = 1024
DEPTH = 4
A_WIDTH = 512
B_WIDTH = 1024
B_HEADS = 8
B_HEAD_DIM = 128
LRU_C = 8.0
M_HEADS = 4
M_QK_DIM = 128
M_V_DIM = 256
M_QK = 512
M_V = 1024
M_CHUNK = 128
X_HEADS = 4
X_HEAD_DIM = 256
N_EXPERTS = 32
TOP_K = 4
D_FF = 1024
SWIGLU_LIMIT = 7.0
SWIGLU_ALPHA = 1.702
DN_ALPHA = (2 * DEPTH) ** 0.25
LN_EPS = 1e-5
RMS_EPS = 1e-6
HALF = D_MODEL // 2
SEG_ALIGN = 8
MXU_DIM = 256

VMEM_LIMIT_BYTES = 56 * 1024 * 1024


def _cparams(n_grid):
    return pltpu.CompilerParams(
        dimension_semantics=("arbitrary",) * n_grid, vmem_limit_bytes=VMEM_LIMIT_BYTES)


def _layer_norm(y, g, b):
    mu = jnp.mean(y, axis=-1, keepdims=True)
    yc = y - mu
    var = jnp.mean(yc * yc, axis=-1, keepdims=True)
    return yc * lax.rsqrt(var + LN_EPS) * g + b


def _sigmoid(x):
    return 1.0 / (1.0 + jnp.exp(-x))


def _softplus(x):
    return jnp.maximum(x, 0.0) + jnp.log1p(jnp.exp(-jnp.abs(x)))


def _gelu_tanh(x):
    return 0.5 * x * (1.0 + jnp.tanh(0.7978845608028654 * (x + 0.044715 * (x * x * x))))


def _split_bf16(x):
    hi = x.astype(BF16)
    lo = (x - hi.astype(F32)).astype(BF16)
    return hi, lo


def _dot(a, b):
    return jnp.dot(a, b, preferred_element_type=F32)


def _dot_nt(a, b):
    return lax.dot_general(a, b, (((1,), (1,)), ((), ())), preferred_element_type=F32)


def _dot_tn(a, b):
    return lax.dot_general(a, b, (((0,), (0,)), ((), ())), preferred_element_type=F32)


def _pack_rows(y):
    bits = lax.bitcast_convert_type(y.astype(BF16).astype(F32), U32)
    return (bits[:, :HALF] >> 16) | (bits[:, HALF:] & jnp.uint32(0xFFFF0000))


def _unpack_rows(w):
    lo = lax.bitcast_convert_type(w << 16, F32)
    hi = lax.bitcast_convert_type(w & jnp.uint32(0xFFFF0000), F32)
    return jnp.concatenate([lo, hi], axis=1)


def _even_kernel(x_ref, win_ref, ca_ref, cb_ref, cbb_ref, wr_ref, br_ref, wi_ref, bi_ref, lam_ref,
                 wout_ref, g_ref, b_ref, o_ref,
                 av_scr, bu_scr, a_scr, b_scr, h_scr, carry_scr, *, ts):
    @pl.when(pl.program_id(1) == 0)
    def _():
        av_scr[0:8, :] = jnp.zeros((8, A_WIDTH), F32)
        bu_scr[0:8, :] = jnp.zeros((8, B_WIDTH), F32)
        carry_scr[...] = jnp.zeros((8, B_WIDTH), F32)

    x = x_ref[0]
    z = _dot(x.astype(BF16), win_ref[...])
    a_b = z[:, 0:A_WIDTH]
    a_c = z[:, A_WIDTH:2 * A_WIDTH]
    a_x = z[:, 2 * A_WIDTH:3 * A_WIDTH]
    b_u = z[:, 3 * A_WIDTH:3 * A_WIDTH + B_WIDTH]
    b_g = z[:, 3 * A_WIDTH + B_WIDTH:]

    av_scr[8:8 + ts, :] = a_c * a_x
    ca = ca_ref[...]
    conv_a = (ca[2:3, :] * av_scr[8:8 + ts, :] + ca[1:2, :] * av_scr[7:7 + ts, :]
              + ca[0:1, :] * av_scr[6:6 + ts, :])
    y_a = a_b * conv_a
    av_scr[0:8, :] = av_scr[ts:ts + 8, :]

    bu_scr[8:8 + ts, :] = b_u
    cb = cb_ref[...]
    u = (cb[3:4, :] * bu_scr[8:8 + ts, :] + cb[2:3, :] * bu_scr[7:7 + ts, :]
         + cb[1:2, :] * bu_scr[6:6 + ts, :] + cb[0:1, :] * bu_scr[5:5 + ts, :] + cbb_ref[...])
    bu_scr[0:8, :] = bu_scr[ts:ts + 8, :]

    ub = u.astype(BF16)
    r_parts, i_parts = [], []
    for h in range(B_HEADS):
        uh = ub[:, h * B_HEAD_DIM:(h + 1) * B_HEAD_DIM]
        r_parts.append(_dot(uh, wr_ref[h]))
        i_parts.append(_dot(uh, wi_ref[h]))
    r = _sigmoid(jnp.concatenate(r_parts, axis=1) + br_ref[...])
    ig = _sigmoid(jnp.concatenate(i_parts, axis=1) + bi_ref[...])
    log_a = (-LRU_C) * r * _softplus(-lam_ref[...])
    a = jnp.exp(log_a)
    a_scr[...] = a
    b_scr[...] = jnp.sqrt(1.0 - a * a) * (ig * u)

    row8 = lax.broadcasted_iota(I32, (8, B_WIDTH), 0)

    def group(i, carry):
        r0 = pl.multiple_of(i * 8, 8)
        ga = a_scr[pl.ds(r0, 8), :]
        gb = b_scr[pl.ds(r0, 8), :]
        for d in (1, 2, 4):
            keep = row8 >= d
            a_sh = jnp.where(keep, pltpu.roll(ga, d, 0), 1.0)
            b_sh = jnp.where(keep, pltpu.roll(gb, d, 0), 0.0)
            gb = ga * b_sh + gb
            ga = ga * a_sh
        hg = gb + ga * carry
        h_scr[pl.ds(r0, 8), :] = hg
        return jnp.broadcast_to(hg[7:8, :], (8, B_WIDTH))

    carry_scr[...] = lax.fori_loop(0, ts // 8, group, carry_scr[...], unroll=4)

    y_b = _gelu_tanh(b_g) * h_scr[...]
    mix = (_dot(y_a.astype(BF16), wout_ref[0:A_WIDTH, :])
           + _dot(y_b.astype(BF16), wout_ref[A_WIDTH:, :]))
    o_ref[0] = _layer_norm(DN_ALPHA * x + mix, g_ref[...], b_ref[...])


def _even_layer(x, w_in, conv_a, conv_b, conv_b_bias, w_r, b_r, w_i, b_i, lam, w_out, g, b, *, ts):
    bsz, s, _ = x.shape
    ev_in = 3 * A_WIDTH + 2 * B_WIDTH
    const2 = lambda bi, si: (0, 0)
    const3 = lambda bi, si: (0, 0, 0)
    return pl.pallas_call(
        functools.partial(_even_kernel, ts=ts),
        grid=(bsz, s // ts),
        in_specs=[
            pl.BlockSpec((1, ts, D_MODEL), lambda bi, si: (bi, si, 0)),
            pl.BlockSpec((D_MODEL, ev_in), const2),
            pl.BlockSpec((3, A_WIDTH), const2),
            pl.BlockSpec((4, B_WIDTH), const2),
            pl.BlockSpec((1, B_WIDTH), const2),
            pl.BlockSpec((B_HEADS, B_HEAD_DIM, B_HEAD_DIM), const3),
            pl.BlockSpec((1, B_WIDTH), const2),
            pl.BlockSpec((B_HEADS, B_HEAD_DIM, B_HEAD_DIM), const3),
            pl.BlockSpec((1, B_WIDTH), const2),
            pl.BlockSpec((1, B_WIDTH), const2),
            pl.BlockSpec((A_WIDTH + B_WIDTH, D_MODEL), const2),
            pl.BlockSpec((1, D_MODEL), const2),
            pl.BlockSpec((1, D_MODEL), const2),
        ],
        out_specs=pl.BlockSpec((1, ts, D_MODEL), lambda bi, si: (bi, si, 0)),
        out_shape=jax.ShapeDtypeStruct(x.shape, F32),
        scratch_shapes=[
            pltpu.VMEM((ts + 8, A_WIDTH), F32),
            pltpu.VMEM((ts + 8, B_WIDTH), F32),
            pltpu.VMEM((ts, B_WIDTH), F32),
            pltpu.VMEM((ts, B_WIDTH), F32),
            pltpu.VMEM((ts, B_WIDTH), F32),
            pltpu.VMEM((8, B_WIDTH), F32),
        ],
        compiler_params=_cparams(2),
        name="even_mixer",
    )(x, w_in.astype(BF16), conv_a, conv_b, conv_b_bias.reshape(1, -1), w_r.astype(BF16),
      b_r.reshape(1, -1), w_i.astype(BF16), b_i.reshape(1, -1), lam.reshape(1, -1),
      w_out.astype(BF16), g.reshape(1, -1), b.reshape(1, -1))


GATE_PAD = 128
V_EXT = M_V_DIM + 128


def _odd_kernel(x_ref, win_ref, wg_ref, bgc_ref, ng_ref, wout_ref, g_ref, b_ref,
                o_ref, c_scr, m_scr, h_scr, *, ts):
    L = M_CHUNK

    @pl.when(pl.program_id(1) == 0)
    def _():
        c_scr[...] = jnp.zeros(c_scr.shape, F32)
        m_scr[...] = jnp.zeros(m_scr.shape, F32)

    x = x_ref[0]
    x_hi, x_lo = _split_bf16(x)
    z = _dot(x_hi, win_ref[...])
    q_all = (z[:, 0:M_QK] * (M_QK_DIM ** -0.5)).astype(BF16)
    k_all = z[:, M_QK:2 * M_QK].astype(BF16)
    v_all = z[:, 2 * M_QK:2 * M_QK + M_V].astype(BF16)
    o_all = z[:, 2 * M_QK + M_V:]

    wg_hi, wg_lo = _split_bf16(wg_ref[...])
    gates_c = _dot(x_hi, wg_hi) + _dot(x_lo, wg_hi) + _dot(x_hi, wg_lo) + bgc_ref[...]
    logf_c = -_softplus(-gates_c)

    rows = lax.broadcasted_iota(I32, (L, L), 0)
    cols = lax.broadcasted_iota(I32, (L, L), 1)
    causal = rows >= cols
    tril = jnp.where(causal, 1.0, 0.0).astype(BF16)
    lane0 = jnp.where(lax.broadcasted_iota(I32, (L, 128), 1) == 0, 1.0, 0.0).astype(BF16)

    for c in range(ts // L):
        sl = slice(c * L, (c + 1) * L)
        fc_hi, fc_lo = _split_bf16(logf_c[sl, :])
        fc_lo2 = (logf_c[sl, :] - fc_hi.astype(F32) - fc_lo.astype(F32)).astype(BF16)
        bcum_c = _dot(tril, fc_hi) + _dot(tril, fc_lo) + _dot(tril, fc_lo2)
        i_minus_b = gates_c[sl, :] - pltpu.roll(bcum_c, GATE_PAD - M_HEADS, 1)
        i_minus_b_t = i_minus_b.T
        for h in range(M_HEADS):
            m_st = m_scr[h][0:1, 0:1]
            b_col = bcum_c[:, M_HEADS + h:M_HEADS + h + 1]
            i_col = gates_c[sl, h:h + 1]
            b_end = b_col[L - 1:L, :]

            d = jnp.where(causal, b_col + i_minus_b_t[h:h + 1, :], -jnp.inf)
            inter = b_col + m_st
            m_t = jnp.maximum(inter, jnp.max(d, axis=1, keepdims=True))
            p = jnp.exp(d - m_t)
            w_inter = jnp.exp(inter - m_t)

            qh = q_all[sl, h * M_QK_DIM:(h + 1) * M_QK_DIM]
            kh = k_all[sl, h * M_QK_DIM:(h + 1) * M_QK_DIM]
            v_ext = jnp.concatenate([v_all[sl, h * M_V_DIM:(h + 1) * M_V_DIM], lane0], axis=1)
            qk = (_dot_nt(qh, kh) * p).astype(BF16)
            c_ext = c_scr[h]
            nd = _dot(qk, v_ext) + w_inter * _dot(qh, c_ext.astype(BF16))
            den = nd[:, M_V_DIM:M_V_DIM + 1]
            hh = nd[:, 0:M_V_DIM] / jnp.maximum(jnp.abs(den), jnp.exp(-m_t))

            g_col = b_end - b_col + i_col
            m_new = jnp.maximum(b_end + m_st, jnp.max(g_col, axis=0, keepdims=True))
            wg_col = jnp.exp(g_col - m_new)
            decay = jnp.exp(b_end + m_st - m_new)
            kv = _dot_tn(kh, (wg_col * v_ext.astype(F32)).astype(BF16))
            c_scr[h] = decay * c_ext + kv
            m_scr[h] = jnp.broadcast_to(m_new, (8, 128))

            hn = hh * lax.rsqrt(jnp.mean(hh * hh, axis=-1, keepdims=True) + RMS_EPS)
            h_scr[sl, h * M_V_DIM:(h + 1) * M_V_DIM] = hn

    gated = _sigmoid(o_all) * (h_scr[...] * ng_ref[...])
    mix = _dot(gated.astype(BF16), wout_ref[...])
    o_ref[0] = _layer_norm(DN_ALPHA * x + mix, g_ref[...], b_ref[...])


def _odd_layer(x, w_in, b_gates, norm_g, w_out, g, b, *, ts):
    bsz, s, _ = x.shape
    n_main = 2 * M_QK + 2 * M_V
    w_main = w_in[:, :n_main].astype(BF16)
    w_gate = w_in[:, n_main:]
    wg_pad = jnp.pad(w_gate, ((0, 0), (0, GATE_PAD - 2 * M_HEADS)))
    bg_col = jnp.pad(b_gates, (0, GATE_PAD - 2 * M_HEADS)).reshape(1, GATE_PAD)
    const2 = lambda bi, si: (0, 0)
    return pl.pallas_call(
        functools.partial(_odd_kernel, ts=ts),
        grid=(bsz, s // ts),
        in_specs=[
            pl.BlockSpec((1, ts, D_MODEL), lambda bi, si: (bi, si, 0)),
            pl.BlockSpec((D_MODEL, n_main), const2),
            pl.BlockSpec((D_MODEL, GATE_PAD), const2),
            pl.BlockSpec((1, GATE_PAD), const2),
            pl.BlockSpec((1, M_V), const2),
            pl.BlockSpec((M_V, D_MODEL), const2),
            pl.BlockSpec((1, D_MODEL), const2),
            pl.BlockSpec((1, D_MODEL), const2),
        ],
        out_specs=pl.BlockSpec((1, ts, D_MODEL), lambda bi, si: (bi, si, 0)),
        out_shape=jax.ShapeDtypeStruct(x.shape, F32),
        scratch_shapes=[
            pltpu.VMEM((M_HEADS, M_QK_DIM, V_EXT), F32),
            pltpu.VMEM((M_HEADS, 8, 128), F32),
            pltpu.VMEM((ts, M_V), F32),
        ],
        compiler_params=_cparams(2),
        name="odd_mixer",
    )(x, w_main, wg_pad, bg_col, norm_g.reshape(1, -1), w_out.astype(BF16),
      g.reshape(1, -1), b.reshape(1, -1))


def _kv_kernel(mem_ref, wk_ref, wv_ref, wq_ref, wo_ref, qk_ref, vo_ref):
    n_mem = mem_ref.shape[1]
    m = mem_ref[0].astype(BF16)
    k = _dot(m, wk_ref[...]).astype(BF16)
    v = _dot(m, wv_ref[...]).astype(BF16)
    for h in range(X_HEADS):
        hs = slice(h * X_HEAD_DIM, (h + 1) * X_HEAD_DIM)
        ms = slice(h * n_mem, (h + 1) * n_mem)
        qk_ref[0, :, ms] = (_dot_nt(wq_ref[:, hs], k[:, hs]) * (X_HEAD_DIM ** -0.5)).astype(BF16)
        vo_ref[0, ms, :] = _dot(v[:, hs], wo_ref[hs, :]).astype(BF16)


def _kv_proj(mem, wk, wv, wq, wo):
    bsz, n_mem, _ = mem.shape
    const2 = lambda bi: (0, 0)
    wblk = pl.BlockSpec((D_MODEL, D_MODEL), const2)
    return pl.pallas_call(
        _kv_kernel,
        grid=(bsz,),
        in_specs=[pl.BlockSpec((1, n_mem, D_MODEL), lambda bi: (bi, 0, 0)), wblk, wblk, wblk, wblk],
        out_specs=[pl.BlockSpec((1, D_MODEL, X_HEADS * n_mem), lambda bi: (bi, 0, 0)),
                   pl.BlockSpec((1, X_HEADS * n_mem, D_MODEL), lambda bi: (bi, 0, 0))],
        out_shape=[jax.ShapeDtypeStruct((bsz, D_MODEL, X_HEADS * n_mem), BF16),
                   jax.ShapeDtypeStruct((bsz, X_HEADS * n_mem, D_MODEL), BF16)],
        compiler_params=_cparams(1),
        name="memory_kv",
    )(mem, wk.astype(BF16), wv.astype(BF16), wq.astype(BF16), wo.astype(BF16))


def _attn_tile(x, qk, vo, g_ref, b_ref, wrt_ref, brt_ref, taken):
    ts = x.shape[0]
    n_mem = qk.shape[1] // X_HEADS
    sc_all = _dot(x.astype(BF16), qk)
    probs = []
    for h in range(X_HEADS):
        sc = sc_all[:, h * n_mem:(h + 1) * n_mem]
        e = jnp.exp(sc - jnp.max(sc, axis=-1, keepdims=True))
        probs.append((e / jnp.sum(e, axis=-1, keepdims=True)).astype(BF16))
    att = _dot(jnp.concatenate(probs, axis=1), vo)
    x2 = _layer_norm(DN_ALPHA * x + att, g_ref[...], b_ref[...])

    x_hi, x_lo = _split_bf16(x2)
    w_hi, w_lo = _split_bf16(wrt_ref[...])
    logits = _dot_nt(w_hi, x_hi) + _dot_nt(w_hi, x_lo) + _dot_nt(w_lo, x_hi) + brt_ref[...]
    e_iota = lax.broadcasted_iota(I32, (N_EXPERTS, ts), 0)
    onehots, vals = [], []
    for _ in range(TOP_K):
        mx = jnp.max(logits, axis=0, keepdims=True)
        sel = jnp.min(jnp.where(logits == mx, e_iota, N_EXPERTS), axis=0, keepdims=True)
        oh = e_iota == sel
        onehots.append(oh)
        vals.append(mx)
        logits = jnp.where(oh, -jnp.inf, logits)
    exps = [jnp.exp(val - vals[0]) for val in vals]
    tot = exps[0] + exps[1] + exps[2] + exps[3]
    tw = jnp.concatenate([ex / tot for ex in exps], axis=0)

    member = jnp.where(onehots[0] | onehots[1] | onehots[2] | onehots[3], 1.0, 0.0)
    t_r = lax.broadcasted_iota(I32, (ts, ts), 0)
    t_c = lax.broadcasted_iota(I32, (ts, ts), 1)
    before = jnp.where(t_r < t_c, 1.0, 0.0).astype(BF16)
    pos = _dot(member.astype(BF16), before)
    n_e = jnp.sum(member, axis=1, keepdims=True)
    q_e = jnp.floor((n_e + (SEG_ALIGN - 1.0)) * (1.0 / SEG_ALIGN))
    e_r = lax.broadcasted_iota(I32, (N_EXPERTS, N_EXPERTS), 0)
    e_c = lax.broadcasted_iota(I32, (N_EXPERTS, N_EXPERTS), 1)
    lower = jnp.where(e_c < e_r, 1.0, 0.0).astype(BF16)
    q_wide = jnp.broadcast_to(q_e, (N_EXPERTS, 128)).astype(BF16)
    lo_e = SEG_ALIGN * _dot(lower, q_wide)[:, 0:1]
    p_e = SEG_ALIGN * q_e
    base = lo_e + pos
    lpos = [jnp.sum(jnp.where(oh, base, 0.0), axis=0, keepdims=True) for oh in onehots]
    lpos = jnp.concatenate(lpos, axis=0).astype(I32)
    lane = lax.broadcasted_iota(I32, (N_EXPERTS, 128), 1)
    meta = jnp.where(lane == 0, p_e, jnp.where(lane == 1, lo_e, jnp.where(lane == 2, taken, 0.0)))
    return x2, x_hi, tw, lpos, meta.astype(I32), taken + p_e


def _attn_kernel(x_ref, qk_ref, vo_ref, g_ref, b_ref, wrt_ref, brt_ref,
                 x2_ref, x2b_ref, tw_ref, lpos_ref, meta_ref, cnt_ref, cnt_scr, *, ts, n_sub):
    @pl.when((pl.program_id(0) == 0) & (pl.program_id(1) == 0))
    def _():
        cnt_scr[...] = jnp.zeros(cnt_scr.shape, F32)

    taken = cnt_scr[...]
    for u in range(n_sub):
        rs = slice(u * ts, (u + 1) * ts)
        x2, x_hi, tw, lpos, meta, taken = _attn_tile(
            x_ref[0, rs, :], qk_ref[0], vo_ref[0], g_ref, b_ref, wrt_ref, brt_ref, taken)
        x2_ref[0, rs, :] = x2
        x2b_ref[0, rs, :] = x_hi
        tw_ref[0, :, rs] = tw
        lpos_ref[0, :, rs] = lpos
        meta_ref[u] = meta
    cnt_scr[...] = taken
    cnt_ref[...] = taken.astype(I32)


def _attn_router(x, qk_mem, vo_mem, g, b, w_router, b_router, *, ts, n_sub):
    bsz, s, _ = x.shape
    n_score = qk_mem.shape[2]
    tg = ts * n_sub
    const2 = lambda bi, si: (0, 0)
    tok = lambda bi, si: (bi, si, 0)
    lanes = lambda bi, si: (bi, 0, si)
    return pl.pallas_call(
        functools.partial(_attn_kernel, ts=ts, n_sub=n_sub),
        grid=(bsz, s // tg),
        in_specs=[
            pl.BlockSpec((1, tg, D_MODEL), tok),
            pl.BlockSpec((1, D_MODEL, n_score), lambda bi, si: (bi, 0, 0)),
            pl.BlockSpec((1, n_score, D_MODEL), lambda bi, si: (bi, 0, 0)),
            pl.BlockSpec((1, D_MODEL), const2),
            pl.BlockSpec((1, D_MODEL), const2),
            pl.BlockSpec((N_EXPERTS, D_MODEL), const2),
            pl.BlockSpec((N_EXPERTS, 1), const2),
        ],
        out_specs=[
            pl.BlockSpec((1, tg, D_MODEL), tok),
            pl.BlockSpec((1, tg, D_MODEL), tok),
            pl.BlockSpec((1, TOP_K, tg), lanes),
            pl.BlockSpec((1, TOP_K, tg), lanes),
            pl.BlockSpec((n_sub, N_EXPERTS, 128), lambda bi, si: (bi * (s // tg) + si, 0, 0)),
            pl.BlockSpec((N_EXPERTS, 128), const2),
        ],
        out_shape=[
            jax.ShapeDtypeStruct((bsz, s, D_MODEL), F32),
            jax.ShapeDtypeStruct((bsz, s, D_MODEL), BF16),
            jax.ShapeDtypeStruct((bsz, TOP_K, s), F32),
            jax.ShapeDtypeStruct((bsz, TOP_K, s), I32),
            jax.ShapeDtypeStruct((bsz * (s // ts), N_EXPERTS, 128), I32),
            jax.ShapeDtypeStruct((N_EXPERTS, 128), I32),
        ],
        scratch_shapes=[pltpu.VMEM((N_EXPERTS, 128), F32)],
        compiler_params=_cparams(2),
        name="memory_attention_router",
    )(x, qk_mem, vo_mem, g.reshape(1, -1), b.reshape(1, -1), w_router.T, b_router.reshape(-1, 1))


WMAT_BLOCK = 16
SEG_CHUNK = 128
SEG_WIDTH = 4 * N_EXPERTS


def _segment_starts(seg_ref, make_copy):
    chunk_bit = SEG_CHUNK.bit_length() - 1

    def per_expert(e, carry):
        n = seg_ref[0, 0, e]
        lo = seg_ref[0, 0, N_EXPERTS + e]
        dst = seg_ref[0, 0, 2 * N_EXPERTS + e]

        def chunk(j, c):
            off = pl.multiple_of(j * SEG_CHUNK, SEG_CHUNK)
            make_copy(pl.multiple_of(lo + off, SEG_ALIGN), pl.multiple_of(dst + off, SEG_ALIGN),
                      SEG_CHUNK).start()
            return c

        lax.fori_loop(0, n >> chunk_bit, chunk, 0)
        for bit in range(chunk_bit - 1, SEG_ALIGN.bit_length() - 2, -1):
            size = 1 << bit
            done = (n >> (bit + 1)) << (bit + 1)

            @pl.when((n & size) != 0)
            def _():
                make_copy(pl.multiple_of(lo + done, SEG_ALIGN),
                          pl.multiple_of(dst + done, SEG_ALIGN), size).start()
        return carry

    lax.fori_loop(0, N_EXPERTS, per_expert, 0)


def _segment_waits(total, make_copy, *, n_sorted):
    for bit in range(n_sorted.bit_length() - 1, SEG_ALIGN.bit_length() - 2, -1):
        size = 1 << bit

        @pl.when((total & size) != 0)
        def _():
            make_copy(0, 0, size).wait()


def _dispatch_kernel(last_ref, nused_ref, total_ref, seg_ref, lpos_ref, x_ref, xs_hbm, sort_scr,
                     zero_scr, sem, zsem, *, tt, tm, n_tiles, n_sorted):
    @pl.when(pl.program_id(0) == 0)
    def _():
        zero_scr[...] = jnp.zeros(zero_scr.shape, U32)

        def zero_copy(row):
            return pltpu.make_async_copy(
                zero_scr, xs_hbm.at[pl.ds(pl.multiple_of(row, tm), tm), :], zsem)

        for e in range(N_EXPERTS):
            @pl.when(last_ref[e] >= 0)
            def _():
                zero_copy(last_ref[e]).start()

        def start_tail(i, carry):
            zero_copy(i * tm).start()
            return carry

        lax.fori_loop(nused_ref[0], n_tiles, start_tail, 0)

        for e in range(N_EXPERTS):
            @pl.when(last_ref[e] >= 0)
            def _():
                zero_copy(last_ref[e]).wait()

        def wait_tail(i, carry):
            zero_copy(i * tm).wait()
            return carry

        lax.fori_loop(nused_ref[0], n_tiles, wait_tail, 0)

    lp = lpos_ref[0]
    r_iota = lax.broadcasted_iota(I32, (n_sorted, tt), 0)
    perm = jnp.where(r_iota == lp[0:1, :], 1.0, 0.0)
    for k in range(1, TOP_K):
        perm = jnp.where(r_iota == lp[k:k + 1, :], 1.0, perm)
    perm = perm.astype(BF16)
    lo_bits = lax.bitcast_convert_type(_dot(perm, x_ref[:, 0:HALF]), U32)
    hi_bits = lax.bitcast_convert_type(_dot(perm, x_ref[:, HALF:]), U32)
    i = pl.program_id(0)
    buf = lax.rem(i, 2)
    sort_scr[buf] = (lo_bits >> 16) | (hi_bits & jnp.uint32(0xFFFF0000))

    def seg_copy_on(slot):
        def seg_copy(lo, dst, size):
            return pltpu.make_async_copy(sort_scr.at[slot, pl.ds(lo, size), :],
                                         xs_hbm.at[pl.ds(dst, size), :], sem.at[slot])
        return seg_copy

    _segment_starts(seg_ref, seg_copy_on(buf))

    @pl.when(i > 0)
    def _():
        _segment_waits(total_ref[i - 1], seg_copy_on(1 - buf), n_sorted=n_sorted)

    @pl.when(i == pl.num_programs(0) - 1)
    def _():
        _segment_waits(total_ref[i], seg_copy_on(buf), n_sorted=n_sorted)


def _dispatch(x2b, lpos_tiles, segs, last_tile_row, n_used, seg_totals, n_rows, *, tt, tm,
              n_sorted):
    n_tok = x2b.shape[0]
    grid_spec = pltpu.PrefetchScalarGridSpec(
        num_scalar_prefetch=3,
        grid=(n_tok // tt,),
        in_specs=[
            pl.BlockSpec((1, 1, SEG_WIDTH), lambda i, la, nu, to: (i, 0, 0),
                         memory_space=pltpu.SMEM),
            pl.BlockSpec((1, TOP_K, tt), lambda i, la, nu, to: (i, 0, 0)),
            pl.BlockSpec((tt, D_MODEL), lambda i, la, nu, to: (i, 0)),
        ],
        out_specs=pl.BlockSpec(memory_space=pl.ANY),
        scratch_shapes=[pltpu.VMEM((2, n_sorted, HALF), U32), pltpu.VMEM((tm, HALF), U32),
                        pltpu.SemaphoreType.DMA((2,)), pltpu.SemaphoreType.DMA],
    )
    return pl.pallas_call(
        functools.partial(_dispatch_kernel, tt=tt, tm=tm, n_tiles=n_rows // tm,
                          n_sorted=n_sorted),
        grid_spec=grid_spec,
        out_shape=jax.ShapeDtypeStruct((n_rows, HALF), U32),
        compiler_params=pltpu.CompilerParams(
            dimension_semantics=("arbitrary",), has_side_effects=True,
            vmem_limit_bytes=VMEM_LIMIT_BYTES),
        name="moe_dispatch",
    )(last_tile_row, n_used, seg_totals, segs, lpos_tiles, x2b)


def _ffn_kernel(te_ref, nused_ref, xs_ref, w1_ref, b1_ref, w2_ref, b2_ref, ys_ref, w1_scr, w2_scr):
    i = pl.program_id(0)
    active = i < nused_ref[0]
    switch = (i == 0) | (te_ref[i] != te_ref[jnp.maximum(i - 1, 0)])

    def ffn(w1g, w1u, w2):
        xt = _unpack_rows(xs_ref[...]).astype(BF16)
        gate = _dot(xt, w1g) + b1_ref[0, :, 0:D_FF]
        up = _dot(xt, w1u) + b1_ref[0, :, D_FF:]
        gate = jnp.minimum(gate, SWIGLU_LIMIT)
        up = jnp.clip(up, -SWIGLU_LIMIT, SWIGLU_LIMIT)
        hid = (up + 1.0) * gate * _sigmoid(SWIGLU_ALPHA * gate)
        ys_ref[...] = _pack_rows(_dot(hid.astype(BF16), w2) + b2_ref[0])

    @pl.when(active & switch)
    def _():
        w1g = w1_ref[0, :, 0:D_FF].astype(BF16)
        w1u = w1_ref[0, :, D_FF:].astype(BF16)
        w2 = w2_ref[0].astype(BF16)
        w1_scr[:, 0:D_FF] = w1g
        w1_scr[:, D_FF:] = w1u
        w2_scr[...] = w2
        ffn(w1g, w1u, w2)

    @pl.when(active & jnp.logical_not(switch))
    def _():
        ffn(w1_scr[:, 0:D_FF], w1_scr[:, D_FF:], w2_scr[...])

    @pl.when(i >= nused_ref[0])
    def _():
        ys_ref[...] = jnp.zeros(ys_ref.shape, U32)


def _grouped_ffn(xs, tile_expert, n_used, w1, b1, w2, b2, *, tm):
    n_rows = xs.shape[0]
    n_tiles = n_rows // tm
    row_blk = lambda i, te, nu: (jnp.minimum(i, nu[0] - 1), 0)
    exp_blk = lambda i, te, nu: (te[i], 0, 0)
    grid_spec = pltpu.PrefetchScalarGridSpec(
        num_scalar_prefetch=2,
        grid=(n_tiles,),
        in_specs=[
            pl.BlockSpec((tm, HALF), row_blk),
            pl.BlockSpec((1, D_MODEL, 2 * D_FF), exp_blk),
            pl.BlockSpec((1, 1, 2 * D_FF), exp_blk),
            pl.BlockSpec((1, D_FF, D_MODEL), exp_blk),
            pl.BlockSpec((1, 1, D_MODEL), exp_blk),
        ],
        out_specs=pl.BlockSpec((tm, HALF), lambda i, te, nu: (i, 0)),
        scratch_shapes=[pltpu.VMEM((D_MODEL, 2 * D_FF), BF16), pltpu.VMEM((D_FF, D_MODEL), BF16)],
    )
    return pl.pallas_call(
        _ffn_kernel,
        grid_spec=grid_spec,
        out_shape=jax.ShapeDtypeStruct((n_rows, HALF), U32),
        compiler_params=_cparams(1),
        name="moe_grouped_ffn",
    )(tile_expert, n_used, xs, w1, b1, w2, b2)


def _combine_kernel(seg_ref, segn_ref, x_ref, lpos_ref, tw_ref, g_ref, b_ref, ys_hbm, o_ref,
                    rows_scr, sem, *, tt, n_sorted):
    i = pl.program_id(0)
    buf = lax.rem(i, 2)

    def seg_copy_on(slot):
        def seg_copy(lo, src, size):
            return pltpu.make_async_copy(ys_hbm.at[pl.ds(src, size), :],
                                         rows_scr.at[slot, pl.ds(lo, size), :], sem.at[slot])
        return seg_copy

    @pl.when(i == 0)
    def _():
        rows_scr[...] = jnp.zeros(rows_scr.shape, U32)
        _segment_starts(seg_ref, seg_copy_on(0))

    @pl.when(i + 1 < pl.num_programs(0))
    def _():
        _segment_starts(segn_ref, seg_copy_on(1 - buf))

    lp = lpos_ref[...]
    tw = tw_ref[...]
    c_iota = lax.broadcasted_iota(I32, (WMAT_BLOCK, n_sorted), 1)
    blocks = []
    for t0 in range(0, tt, WMAT_BLOCK):
        lp0 = lp[t0:t0 + WMAT_BLOCK, :]
        tw0 = tw[t0:t0 + WMAT_BLOCK, :]
        blk = jnp.where(c_iota == lp0[:, 0:1], tw0[:, 0:1], 0.0)
        for k in range(1, TOP_K):
            blk = jnp.where(c_iota == lp0[:, k:k + 1], tw0[:, k:k + 1], blk)
        blocks.append(blk.astype(BF16))
    wmat = jnp.concatenate(blocks, axis=0)

    _segment_waits(seg_ref[0, 0, 3 * N_EXPERTS], seg_copy_on(buf), n_sorted=n_sorted)

    rows = rows_scr[buf]
    r_lo = lax.bitcast_convert_type(rows << 16, F32).astype(BF16)
    r_hi = lax.bitcast_convert_type(rows & jnp.uint32(0xFFFF0000), F32).astype(BF16)
    y = jnp.concatenate([_dot(wmat, r_lo), _dot(wmat, r_hi)], axis=1)
    o_ref[...] = _layer_norm(DN_ALPHA * x_ref[...] + y, g_ref[...], b_ref[...])


def _combine(x2, ys, segs, lpos_tok, tw_tok, g, b, *, tt, n_sorted):
    n_tok = x2.shape[0]
    const2 = lambda i: (0, 0)
    return pl.pallas_call(
        functools.partial(_combine_kernel, tt=tt, n_sorted=n_sorted),
        grid=(n_tok // tt,),
        in_specs=[
            pl.BlockSpec((1, 1, SEG_WIDTH), lambda i: (i, 0, 0), memory_space=pltpu.SMEM),
            pl.BlockSpec((1, 1, SEG_WIDTH), lambda i: (jnp.minimum(i + 1, n_tok // tt - 1), 0, 0),
                         memory_space=pltpu.SMEM),
            pl.BlockSpec((tt, D_MODEL), lambda i: (i, 0)),
            pl.BlockSpec((tt, TOP_K), lambda i: (i, 0)),
            pl.BlockSpec((tt, TOP_K), lambda i: (i, 0)),
            pl.BlockSpec((1, D_MODEL), const2),
            pl.BlockSpec((1, D_MODEL), const2),
            pl.BlockSpec(memory_space=pl.ANY),
        ],
        out_specs=pl.BlockSpec((tt, D_MODEL), lambda i: (i, 0)),
        out_shape=jax.ShapeDtypeStruct((n_tok, D_MODEL), F32),
        scratch_shapes=[pltpu.VMEM((2, n_sorted, HALF), U32), pltpu.SemaphoreType.DMA((2,))],
        compiler_params=_cparams(1),
        name="moe_combine",
    )(segs, segs, x2, lpos_tok, tw_tok, g.reshape(1, -1), b.reshape(1, -1), ys)


def _moe_layer(x2, x2b, tw, lpos, meta, counts, w1, b1, w2, b2, g, b, layer, *, tm, tt):
    bsz, s, _ = x2.shape
    n_tok = bsz * s
    n_tok_tiles = n_tok // tt
    seg_pad = N_EXPERTS * (SEG_ALIGN - 1)
    n_sorted = -(-(TOP_K * tt + seg_pad) // MXU_DIM) * MXU_DIM
    n_tiles = -(-(n_tok * TOP_K + n_tok_tiles * seg_pad) // tm) + N_EXPERTS
    n_rows = n_tiles * tm

    tiles_e = (counts + tm - 1) // tm
    tile_end = jnp.cumsum(tiles_e)
    start_e = (tile_end - tiles_e) * tm
    n_used = tile_end[-1]
    last_tile_row = jnp.where(tiles_e > 0, (tile_end - 1) * tm, -1).astype(I32)
    n_used_arr = n_used.reshape(1).astype(I32)
    tile_ids = jnp.minimum(jnp.arange(n_tiles, dtype=I32), n_used - 1)
    tile_expert = jnp.sum(tile_ids[:, None] >= tile_end[None, :], axis=1).astype(I32)
    seg_len, seg_lo = meta[:, :, 0], meta[:, :, 1]
    seg_total = jnp.broadcast_to(seg_lo[:, -1:] + seg_len[:, -1:], (n_tok_tiles, N_EXPERTS))
    segs = jnp.concatenate([seg_len, seg_lo, meta[:, :, 2] + start_e[None, :], seg_total], axis=1)
    segs = segs.reshape(n_tok_tiles, 1, SEG_WIDTH).astype(I32)
    lpos_tiles = lpos.reshape(bsz, TOP_K, s // tt, tt).transpose(0, 2, 1, 3)
    lpos_tiles = lpos_tiles.reshape(n_tok_tiles, TOP_K, tt)

    xs = _dispatch(x2b.reshape(n_tok, D_MODEL), lpos_tiles, segs, last_tile_row, n_used_arr,
                   seg_total[:, 0].astype(I32), n_rows, tt=tt, tm=tm, n_sorted=n_sorted)
    ys = _grouped_ffn(xs, tile_expert + layer * N_EXPERTS, n_used_arr, w1, b1, w2, b2, tm=tm)
    lpos_tok = lpos.transpose(0, 2, 1).reshape(n_tok, TOP_K)
    tw_tok = tw.transpose(0, 2, 1).reshape(n_tok, TOP_K)
    out = _combine(x2.reshape(n_tok, D_MODEL), ys, segs, lpos_tok, tw_tok, g, b, tt=tt,
                   n_sorted=n_sorted)
    return out.reshape(bsz, s, D_MODEL)


def _pick_tile(n, want):
    t = min(n, want)
    while n % t:
        t //= 2
    return t


def kernel(x, mem, ev_w_in, ev_conv_a, ev_conv_b, ev_conv_b_bias, ev_w_rgate, ev_b_rgate, ev_w_igate, ev_b_igate, ev_lambda, ev_w_out, od_w_in, od_b_gates, od_norm_g, od_w_out, xa_wq, xa_wk, xa_wv, xa_wo, moe_w_router, moe_b_router, moe_w1, moe_b1, moe_w2, moe_b2, ln_g, ln_b):
    bsz, s, _ = x.shape
    ts_even = _pick_tile(s, 512)
    ts_odd = _pick_tile(s, 256)
    ts_attn = _pick_tile(s, 512)
    n_sub_attn = 2 if s % (2 * ts_attn) == 0 else 1
    tm = 512

    w1_all = moe_w1.reshape(DEPTH * N_EXPERTS, D_MODEL, 2 * D_FF)
    w2_all = moe_w2.reshape(DEPTH * N_EXPERTS, D_FF, D_MODEL)
    b1_all = moe_b1.reshape(DEPTH * N_EXPERTS, 1, 2 * D_FF)
    b2_all = moe_b2.reshape(DEPTH * N_EXPERTS, 1, D_MODEL)
    for layer in range(DEPTH):
        j = layer // 2
        if layer % 2 == 0:
            x = _even_layer(x, ev_w_in[j], ev_conv_a[j], ev_conv_b[j], ev_conv_b_bias[j],
                            ev_w_rgate[j], ev_b_rgate[j], ev_w_igate[j], ev_b_igate[j],
                            ev_lambda[j], ev_w_out[j], ln_g[layer, 0], ln_b[layer, 0], ts=ts_even)
        else:
            x = _odd_layer(x, od_w_in[j], od_b_gates[j], od_norm_g[j], od_w_out[j],
                           ln_g[layer, 0], ln_b[layer, 0], ts=ts_odd)
        qk_mem, vo_mem = _kv_proj(mem, xa_wk[layer], xa_wv[layer], xa_wq[layer], xa_wo[layer])
        x2, x2b, tw, lpos, meta, cnt = _attn_router(
            x, qk_mem, vo_mem, ln_g[layer, 1], ln_b[layer, 1],
            moe_w_router[layer], moe_b_router[layer], ts=ts_attn, n_sub=n_sub_attn)
        x = _moe_layer(x2, x2b, tw, lpos, meta, cnt[:, 0], w1_all, b1_all, w2_all, b2_all,
                       ln_g[layer, 2], ln_b[layer, 2], layer, tm=tm, tt=ts_attn)
    return x
```

```python
import functools

import jax
import jax.numpy as jnp
from jax import lax
from jax.experimental import pallas as pl
from jax.experimental.pallas import tpu as pltpu

F32 = jnp.float32
BF16 = jnp.bfloat16
U32 = jnp.uint32
I32 = jnp.int32

D_MODEL = 1024
DEPTH = 4
A_WIDTH = 512
B_WIDTH = 1024
B_HEADS = 8
B_HEAD_DIM = 128
LRU_C = 8.0
M_HEADS = 4
M_QK_DIM = 128
M_V_DIM = 256
M_QK = 512
M_V = 1024
M_CHUNK = 128
X_HEADS = 4
X_HEAD_DIM = 256
N_EXPERTS = 32
TOP_K = 4
D_FF = 1024
SWIGLU_LIMIT = 7.0
SWIGLU_ALPHA = 1.702
DN_ALPHA = (2 * DEPTH) ** 0.25
LN_EPS = 1e-5
RMS_EPS = 1e-6
HALF = D_MODEL // 2
SEG_ALIGN = 8
MXU_DIM = 256

VMEM_LIMIT_BYTES = 56 * 1024 * 1024


def _cparams(n_grid):
    return pltpu.CompilerParams(
        dimension_semantics=("arbitrary",) * n_grid, vmem_limit_bytes=VMEM_LIMIT_BYTES)


def _layer_norm(y, g, b):
    mu = jnp.mean(y, axis=-1, keepdims=True)
    yc = y - mu
    var = jnp.mean(yc * yc, axis=-1, keepdims=True)
    return yc * lax.rsqrt(var + LN_EPS) * g + b


def _sigmoid(x):
    return 1.0 / (1.0 + jnp.exp(-x))


def _softplus(x):
    return jnp.maximum(x, 0.0) + jnp.log1p(jnp.exp(-jnp.abs(x)))


def _gelu_tanh(x):
    return 0.5 * x * (1.0 + jnp.tanh(0.7978845608028654 * (x + 0.044715 * (x * x * x))))


def _split_bf16(x):
    hi = x.astype(BF16)
    lo = (x - hi.astype(F32)).astype(BF16)
    return hi, lo


def _dot(a, b):
    return jnp.dot(a, b, preferred_element_type=F32)


def _dot_nt(a, b):
    return lax.dot_general(a, b, (((1,), (1,)), ((), ())), preferred_element_type=F32)


def _dot_tn(a, b):
    return lax.dot_general(a, b, (((0,), (0,)), ((), ())), preferred_element_type=F32)


def _pack_rows(y):
    bits = lax.bitcast_convert_type(y.astype(BF16).astype(F32), U32)
    return (bits[:, :HALF] >> 16) | (bits[:, HALF:] & jnp.uint32(0xFFFF0000))


def _unpack_rows(w):
    lo = lax.bitcast_convert_type(w << 16, F32)
    hi = lax.bitcast_convert_type(w & jnp.uint32(0xFFFF0000), F32)
    return jnp.concatenate([lo, hi], axis=1)


def _even_kernel(x_ref, win_ref, ca_ref, cb_ref, cbb_ref, wr_ref, br_ref, wi_ref, bi_ref, lam_ref,
                 wout_ref, g_ref, b_ref, o_ref,
                 av_scr, bu_scr, a_scr, b_scr, h_scr, carry_scr, *, ts):
    @pl.when(pl.program_id(1) == 0)
    def _():
        av_scr[0:8, :] = jnp.zeros((8, A_WIDTH), F32)
        bu_scr[0:8, :] = jnp.zeros((8, B_WIDTH), F32)
        carry_scr[...] = jnp.zeros((8, B_WIDTH), F32)

    x = x_ref[0]
    z = _dot(x.astype(BF16), win_ref[...])
    a_b = z[:, 0:A_WIDTH]
    a_c = z[:, A_WIDTH:2 * A_WIDTH]
    a_x = z[:, 2 * A_WIDTH:3 * A_WIDTH]
    b_u = z[:, 3 * A_WIDTH:3 * A_WIDTH + B_WIDTH]
    b_g = z[:, 3 * A_WIDTH + B_WIDTH:]

    av_scr[8:8 + ts, :] = a_c * a_x
    ca = ca_ref[...]
    conv_a = (ca[2:3, :] * av_scr[8:8 + ts, :] + ca[1:2, :] * av_scr[7:7 + ts, :]
              + ca[0:1, :] * av_scr[6:6 + ts, :])
    y_a = a_b * conv_a
    av_scr[0:8, :] = av_scr[ts:ts + 8, :]

    bu_scr[8:8 + ts, :] = b_u
    cb = cb_ref[...]
    u = (cb[3:4, :] * bu_scr[8:8 + ts, :] + cb[2:3, :] * bu_scr[7:7 + ts, :]
         + cb[1:2, :] * bu_scr[6:6 + ts, :] + cb[0:1, :] * bu_scr[5:5 + ts, :] + cbb_ref[...])
    bu_scr[0:8, :] = bu_scr[ts:ts + 8, :]

    ub = u.astype(BF16)
    r_parts, i_parts = [], []
    for h in range(B_HEADS):
        uh = ub[:, h * B_HEAD_DIM:(h + 1) * B_HEAD_DIM]
        r_parts.append(_dot(uh, wr_ref[h]))
        i_parts.append(_dot(uh, wi_ref[h]))
    r = _sigmoid(jnp.concatenate(r_parts, axis=1) + br_ref[...])
    ig = _sigmoid(jnp.concatenate(i_parts, axis=1) + bi_ref[...])
    log_a = (-LRU_C) * r * _softplus(-lam_ref[...])
    a = jnp.exp(log_a)
    a_scr[...] = a
    b_scr[...] = jnp.sqrt(1.0 - a * a) * (ig * u)

    row8 = lax.broadcasted_iota(I32, (8, B_WIDTH), 0)

    def group(i, carry):
        r0 = pl.multiple_of(i * 8, 8)
        ga = a_scr[pl.ds(r0, 8), :]
        gb = b_scr[pl.ds(r0, 8), :]
        for d in (1, 2, 4):
            keep = row8 >= d
            a_sh = jnp.where(keep, pltpu.roll(ga, d, 0), 1.0)
            b_sh = jnp.where(keep, pltpu.roll(gb, d, 0), 0.0)
            gb = ga * b_sh + gb
            ga = ga * a_sh
        hg = gb + ga * carry
        h_scr[pl.ds(r0, 8), :] = hg
        return jnp.broadcast_to(hg[7:8, :], (8, B_WIDTH))

    carry_scr[...] = lax.fori_loop(0, ts // 8, group, carry_scr[...], unroll=4)

    y_b = _gelu_tanh(b_g) * h_scr[...]
    mix = (_dot(y_a.astype(BF16), wout_ref[0:A_WIDTH, :])
           + _dot(y_b.astype(BF16), wout_ref[A_WIDTH:, :]))
    o_ref[0] = _layer_norm(DN_ALPHA * x + mix, g_ref[...], b_ref[...])


def _even_layer(x, w_in, conv_a, conv_b, conv_b_bias, w_r, b_r, w_i, b_i, lam, w_out, g, b, *, ts):
    bsz, s, _ = x.shape
    ev_in = 3 * A_WIDTH + 2 * B_WIDTH
    const2 = lambda bi, si: (0, 0)
    const3 = lambda bi, si: (0, 0, 0)
    return pl.pallas_call(
        functools.partial(_even_kernel, ts=ts),
        grid=(bsz, s // ts),
        in_specs=[
            pl.BlockSpec((1, ts, D_MODEL), lambda bi, si: (bi, si, 0)),
            pl.BlockSpec((D_MODEL, ev_in), const2),
            pl.BlockSpec((3, A_WIDTH), const2),
            pl.BlockSpec((4, B_WIDTH), const2),
            pl.BlockSpec((1, B_WIDTH), const2),
            pl.BlockSpec((B_HEADS, B_HEAD_DIM, B_HEAD_DIM), const3),
            pl.BlockSpec((1, B_WIDTH), const2),
            pl.BlockSpec((B_HEADS, B_HEAD_DIM, B_HEAD_DIM), const3),
            pl.BlockSpec((1, B_WIDTH), const2),
            pl.BlockSpec((1, B_WIDTH), const2),
            pl.BlockSpec((A_WIDTH + B_WIDTH, D_MODEL), const2),
            pl.BlockSpec((1, D_MODEL), const2),
            pl.BlockSpec((1, D_MODEL), const2),
        ],
        out_specs=pl.BlockSpec((1, ts, D_MODEL), lambda bi, si: (bi, si, 0)),
        out_shape=jax.ShapeDtypeStruct(x.shape, F32),
        scratch_shapes=[
            pltpu.VMEM((ts + 8, A_WIDTH), F32),
            pltpu.VMEM((ts + 8, B_WIDTH), F32),
            pltpu.VMEM((ts, B_WIDTH), F32),
            pltpu.VMEM((ts, B_WIDTH), F32),
            pltpu.VMEM((ts, B_WIDTH), F32),
            pltpu.VMEM((8, B_WIDTH), F32),
        ],
        compiler_params=_cparams(2),
        name="even_mixer",
    )(x, w_in.astype(BF16), conv_a, conv_b, conv_b_bias.reshape(1, -1), w_r.astype(BF16),
      b_r.reshape(1, -1), w_i.astype(BF16), b_i.reshape(1, -1), lam.reshape(1, -1),
      w_out.astype(BF16), g.reshape(1, -1), b.reshape(1, -1))


GATE_PAD = 128
V_EXT = M_V_DIM + 128


def _odd_kernel(x_ref, win_ref, wg_ref, bgc_ref, ng_ref, wout_ref, g_ref, b_ref,
                o_ref, c_scr, m_scr, h_scr, *, ts):
    L = M_CHUNK

    @pl.when(pl.program_id(1) == 0)
    def _():
        c_scr[...] = jnp.zeros(c_scr.shape, F32)
        m_scr[...] = jnp.zeros(m_scr.shape, F32)

    x = x_ref[0]
    x_hi, x_lo = _split_bf16(x)
    z = _dot(x_hi, win_ref[...])
    q_all = (z[:, 0:M_QK] * (M_QK_DIM ** -0.5)).astype(BF16)
    k_all = z[:, M_QK:2 * M_QK].astype(BF16)
    v_all = z[:, 2 * M_QK:2 * M_QK + M_V].astype(BF16)
    o_all = z[:, 2 * M_QK + M_V:]

    wg_hi, wg_lo = _split_bf16(wg_ref[...])
    both = _dot(x_hi, jnp.concatenate([wg_hi, wg_lo], axis=1))
    gates_c = both[:, 0:GATE_PAD] + _dot(x_lo, wg_hi) + both[:, GATE_PAD:] + bgc_ref[...]
    logf_c = -_softplus(-gates_c)

    rows = lax.broadcasted_iota(I32, (L, L), 0)
    cols = lax.broadcasted_iota(I32, (L, L), 1)
    causal = rows >= cols
    tril = jnp.where(causal, 1.0, 0.0).astype(BF16)
    lane0 = jnp.where(lax.broadcasted_iota(I32, (L, 128), 1) == 0, 1.0, 0.0).astype(BF16)

    for c in range(ts // L):
        sl = slice(c * L, (c + 1) * L)
        fc_hi, fc_lo = _split_bf16(logf_c[sl, :])
        fc_lo2 = (logf_c[sl, :] - fc_hi.astype(F32) - fc_lo.astype(F32)).astype(BF16)
        bcum_c = _dot(tril, fc_hi) + _dot(tril, fc_lo) + _dot(tril, fc_lo2)
        i_minus_b = gates_c[sl, :] - pltpu.roll(bcum_c, GATE_PAD - M_HEADS, 1)
        i_minus_b_t = i_minus_b.T
        for h in range(M_HEADS):
            m_st = m_scr[h][0:1, 0:1]
            b_col = bcum_c[:, M_HEADS + h:M_HEADS + h + 1]
            i_col = gates_c[sl, h:h + 1]
            b_end = b_col[L - 1:L, :]

            d = jnp.where(causal, b_col + i_minus_b_t[h:h + 1, :], -jnp.inf)
            inter = b_col + m_st
            m_t = jnp.maximum(inter, jnp.max(d, axis=1, keepdims=True))
            p = jnp.exp(d - m_t)
            w_inter = jnp.exp(inter - m_t)

            qh = q_all[sl, h * M_QK_DIM:(h + 1) * M_QK_DIM]
            kh = k_all[sl, h * M_QK_DIM:(h + 1) * M_QK_DIM]
            v_ext = jnp.concatenate([v_all[sl, h * M_V_DIM:(h + 1) * M_V_DIM], lane0], axis=1)
            qk = (_dot_nt(qh, kh) * p).astype(BF16)
            c_ext = c_scr[h]
            nd = _dot(qk, v_ext) + w_inter * _dot(qh, c_ext.astype(BF16))
            den = nd[:, M_V_DIM:M_V_DIM + 1]
            hh = nd[:, 0:M_V_DIM] / jnp.maximum(jnp.abs(den), jnp.exp(-m_t))

            g_col = b_end - b_col + i_col
            m_new = jnp.maximum(b_end + m_st, jnp.max(g_col, axis=0, keepdims=True))
            wg_col = jnp.exp(g_col - m_new)
            decay = jnp.exp(b_end + m_st - m_new)
            kv = _dot_tn(kh, (wg_col * v_ext.astype(F32)).astype(BF16))
            c_scr[h] = decay * c_ext + kv
            m_scr[h] = jnp.broadcast_to(m_new, (8, 128))

            hn = hh * lax.rsqrt(jnp.mean(hh * hh, axis=-1, keepdims=True) + RMS_EPS)
            h_scr[sl, h * M_V_DIM:(h + 1) * M_V_DIM] = hn

    gated = _sigmoid(o_all) * (h_scr[...] * ng_ref[...])
    mix = _dot(gated.astype(BF16), wout_ref[...])
    o_ref[0] = _layer_norm(DN_ALPHA * x + mix, g_ref[...], b_ref[...])


def _odd_layer(x, w_in, b_gates, norm_g, w_out, g, b, *, ts):
    bsz, s, _ = x.shape
    n_main = 2 * M_QK + 2 * M_V
    w_main = w_in[:, :n_main].astype(BF16)
    w_gate = w_in[:, n_main:]
    wg_pad = jnp.pad(w_gate, ((0, 0), (0, GATE_PAD - 2 * M_HEADS)))
    bg_col = jnp.pad(b_gates, (0, GATE_PAD - 2 * M_HEADS)).reshape(1, GATE_PAD)
    const2 = lambda bi, si: (0, 0)
    return pl.pallas_call(
        functools.partial(_odd_kernel, ts=ts),
        grid=(bsz, s // ts),
        in_specs=[
            pl.BlockSpec((1, ts, D_MODEL), lambda bi, si: (bi, si, 0)),
            pl.BlockSpec((D_MODEL, n_main), const2),
            pl.BlockSpec((D_MODEL, GATE_PAD), const2),
            pl.BlockSpec((1, GATE_PAD), const2),
            pl.BlockSpec((1, M_V), const2),
            pl.BlockSpec((M_V, D_MODEL), const2),
            pl.BlockSpec((1, D_MODEL), const2),
            pl.BlockSpec((1, D_MODEL), const2),
        ],
        out_specs=pl.BlockSpec((1, ts, D_MODEL), lambda bi, si: (bi, si, 0)),
        out_shape=jax.ShapeDtypeStruct(x.shape, F32),
        scratch_shapes=[
            pltpu.VMEM((M_HEADS, M_QK_DIM, V_EXT), F32),
            pltpu.VMEM((M_HEADS, 8, 128), F32),
            pltpu.VMEM((ts, M_V), F32),
        ],
        compiler_params=_cparams(2),
        name="odd_mixer",
    )(x, w_main, wg_pad, bg_col, norm_g.reshape(1, -1), w_out.astype(BF16),
      g.reshape(1, -1), b.reshape(1, -1))


def _kv_kernel(mem_ref, wk_ref, wv_ref, wq_ref, wo_ref, qk_ref, vo_ref):
    n_mem = mem_ref.shape[1]
    m = mem_ref[0].astype(BF16)
    k = _dot(m, wk_ref[...]).astype(BF16)
    v = _dot(m, wv_ref[...]).astype(BF16)
    for h in range(X_HEADS):
        hs = slice(h * X_HEAD_DIM, (h + 1) * X_HEAD_DIM)
        ms = slice(h * n_mem, (h + 1) * n_mem)
        qk_ref[0, :, ms] = (_dot_nt(wq_ref[:, hs], k[:, hs]) * (X_HEAD_DIM ** -0.5)).astype(BF16)
        vo_ref[0, ms, :] = _dot(v[:, hs], wo_ref[hs, :]).astype(BF16)


def _kv_proj(mem, wk, wv, wq, wo):
    bsz, n_mem, _ = mem.shape
    const2 = lambda bi: (0, 0)
    wblk = pl.BlockSpec((D_MODEL, D_MODEL), const2)
    return pl.pallas_call(
        _kv_kernel,
        grid=(bsz,),
        in_specs=[pl.BlockSpec((1, n_mem, D_MODEL), lambda bi: (bi, 0, 0)), wblk, wblk, wblk, wblk],
        out_specs=[pl.BlockSpec((1, D_MODEL, X_HEADS * n_mem), lambda bi: (bi, 0, 0)),
                   pl.BlockSpec((1, X_HEADS * n_mem, D_MODEL), lambda bi: (bi, 0, 0))],
        out_shape=[jax.ShapeDtypeStruct((bsz, D_MODEL, X_HEADS * n_mem), BF16),
                   jax.ShapeDtypeStruct((bsz, X_HEADS * n_mem, D_MODEL), BF16)],
        compiler_params=_cparams(1),
        name="memory_kv",
    )(mem, wk.astype(BF16), wv.astype(BF16), wq.astype(BF16), wo.astype(BF16))


def _attn_tile(x, qk, vo, g_ref, b_ref, wrt_ref, brt_ref, taken):
    ts = x.shape[0]
    n_mem = qk.shape[1] // X_HEADS
    sc_all = _dot(x.astype(BF16), qk)
    probs = []
    for h in range(X_HEADS):
        sc = sc_all[:, h * n_mem:(h + 1) * n_mem]
        e = jnp.exp(sc - jnp.max(sc, axis=-1, keepdims=True))
        probs.append((e / jnp.sum(e, axis=-1, keepdims=True)).astype(BF16))
    att = _dot(jnp.concatenate(probs, axis=1), vo)
    x2 = _layer_norm(DN_ALPHA * x + att, g_ref[...], b_ref[...])

    x_hi, x_lo = _split_bf16(x2)
    w_hi, w_lo = _split_bf16(wrt_ref[...])
    both = _dot_nt(jnp.concatenate([w_hi, w_lo], axis=0), x_hi)
    logits = both[0:N_EXPERTS, :] + _dot_nt(w_hi, x_lo) + both[N_EXPERTS:, :] + brt_ref[...]
    e_iota = lax.broadcasted_iota(I32, (N_EXPERTS, ts), 0)
    onehots, vals = [], []
    for _ in range(TOP_K):
        mx = jnp.max(logits, axis=0, keepdims=True)
        sel = jnp.min(jnp.where(logits == mx, e_iota, N_EXPERTS), axis=0, keepdims=True)
        oh = e_iota == sel
        onehots.append(oh)
        vals.append(mx)
        logits = jnp.where(oh, -jnp.inf, logits)
    exps = [jnp.exp(val - vals[0]) for val in vals]
    tot = exps[0] + exps[1] + exps[2] + exps[3]
    tw = jnp.concatenate([ex / tot for ex in exps], axis=0)

    member = jnp.where(onehots[0] | onehots[1] | onehots[2] | onehots[3], 1.0, 0.0)
    t_r = lax.broadcasted_iota(I32, (ts, ts), 0)
    t_c = lax.broadcasted_iota(I32, (ts, ts), 1)
    before = jnp.where(t_r < t_c, 1.0, 0.0).astype(BF16)
    pos = _dot(member.astype(BF16), before)
    n_e = jnp.sum(member, axis=1, keepdims=True)
    q_e = jnp.floor((n_e + (SEG_ALIGN - 1.0)) * (1.0 / SEG_ALIGN))
    e_r = lax.broadcasted_iota(I32, (N_EXPERTS, N_EXPERTS), 0)
    e_c = lax.broadcasted_iota(I32, (N_EXPERTS, N_EXPERTS), 1)
    lower = jnp.where(e_c < e_r, 1.0, 0.0).astype(BF16)
    q_wide = jnp.broadcast_to(q_e, (N_EXPERTS, 128)).astype(BF16)
    lo_e = SEG_ALIGN * _dot(lower, q_wide)[:, 0:1]
    p_e = SEG_ALIGN * q_e
    base = lo_e + pos
    lpos = [jnp.sum(jnp.where(oh, base, 0.0), axis=0, keepdims=True) for oh in onehots]
    lpos = jnp.concatenate(lpos, axis=0).astype(I32)
    lane = lax.broadcasted_iota(I32, (N_EXPERTS, 128), 1)
    meta = jnp.where(lane == 0, p_e, jnp.where(lane == 1, lo_e, jnp.where(lane == 2, taken, 0.0)))
    return x2, x_hi, tw, lpos, meta.astype(I32), taken + p_e


def _attn_kernel(x_ref, qk_ref, vo_ref, g_ref, b_ref, wrt_ref, brt_ref,
                 x2_ref, x2b_ref, tw_ref, lpos_ref, meta_ref, cnt_ref, cnt_scr, *, ts, n_sub):
    @pl.when((pl.program_id(0) == 0) & (pl.program_id(1) == 0))
    def _():
        cnt_scr[...] = jnp.zeros(cnt_scr.shape, F32)

    taken = cnt_scr[...]
    for u in range(n_sub):
        rs = slice(u * ts, (u + 1) * ts)
        x2, x_hi, tw, lpos, meta, taken = _attn_tile(
            x_ref[0, rs, :], qk_ref[0], vo_ref[0], g_ref, b_ref, wrt_ref, brt_ref, taken)
        x2_ref[0, rs, :] = x2
        x2b_ref[0, rs, :] = x_hi
        tw_ref[0, :, rs] = tw
        lpos_ref[0, :, rs] = lpos
        meta_ref[u] = meta
    cnt_scr[...] = taken
    cnt_ref[...] = taken.astype(I32)


def _attn_router(x, qk_mem, vo_mem, g, b, w_router, b_router, *, ts, n_sub):
    bsz, s, _ = x.shape
    n_score = qk_mem.shape[2]
    tg = ts * n_sub
    const2 = lambda bi, si: (0, 0)
    tok = lambda bi, si: (bi, si, 0)
    lanes = lambda bi, si: (bi, 0, si)
    return pl.pallas_call(
        functools.partial(_attn_kernel, ts=ts, n_sub=n_sub),
        grid=(bsz, s // tg),
        in_specs=[
            pl.BlockSpec((1, tg, D_MODEL), tok),
            pl.BlockSpec((1, D_MODEL, n_score), lambda bi, si: (bi, 0, 0)),
            pl.BlockSpec((1, n_score, D_MODEL), lambda bi, si: (bi, 0, 0)),
            pl.BlockSpec((1, D_MODEL), const2),
            pl.BlockSpec((1, D_MODEL), const2),
            pl.BlockSpec((N_EXPERTS, D_MODEL), const2),
            pl.BlockSpec((N_EXPERTS, 1), const2),
        ],
        out_specs=[
            pl.BlockSpec((1, tg, D_MODEL), tok),
            pl.BlockSpec((1, tg, D_MODEL), tok),
            pl.BlockSpec((1, TOP_K, tg), lanes),
            pl.BlockSpec((1, TOP_K, tg), lanes),
            pl.BlockSpec((n_sub, N_EXPERTS, 128), lambda bi, si: (bi * (s // tg) + si, 0, 0)),
            pl.BlockSpec((N_EXPERTS, 128), const2),
        ],
        out_shape=[
            jax.ShapeDtypeStruct((bsz, s, D_MODEL), F32),
            jax.ShapeDtypeStruct((bsz, s, D_MODEL), BF16),
            jax.ShapeDtypeStruct((bsz, TOP_K, s), F32),
            jax.ShapeDtypeStruct((bsz, TOP_K, s), I32),
            jax.ShapeDtypeStruct((bsz * (s // ts), N_EXPERTS, 128), I32),
            jax.ShapeDtypeStruct((N_EXPERTS, 128), I32),
        ],
        scratch_shapes=[pltpu.VMEM((N_EXPERTS, 128), F32)],
        compiler_params=_cparams(2),
        name="memory_attention_router",
    )(x, qk_mem, vo_mem, g.reshape(1, -1), b.reshape(1, -1), w_router.T, b_router.reshape(-1, 1))


WMAT_BLOCK = 16
SEG_CHUNK = 128
SEG_WIDTH = 4 * N_EXPERTS


def _segment_starts(seg_ref, make_copy):
    chunk_bit = SEG_CHUNK.bit_length() - 1

    def per_expert(e, carry):
        n = seg_ref[0, 0, e]
        lo = seg_ref[0, 0, N_EXPERTS + e]
        dst = seg_ref[0, 0, 2 * N_EXPERTS + e]

        def chunk(j, c):
            off = pl.multiple_of(j * SEG_CHUNK, SEG_CHUNK)
            make_copy(pl.multiple_of(lo + off, SEG_ALIGN), pl.multiple_of(dst + off, SEG_ALIGN),
                      SEG_CHUNK).start()
            return c

        lax.fori_loop(0, n >> chunk_bit, chunk, 0)
        for bit in range(chunk_bit - 1, SEG_ALIGN.bit_length() - 2, -1):
            size = 1 << bit
            done = (n >> (bit + 1)) << (bit + 1)

            @pl.when((n & size) != 0)
            def _():
                make_copy(pl.multiple_of(lo + done, SEG_ALIGN),
                          pl.multiple_of(dst + done, SEG_ALIGN), size).start()
        return carry

    lax.fori_loop(0, N_EXPERTS, per_expert, 0)


def _segment_waits(total, make_copy, *, n_sorted):
    for bit in range(n_sorted.bit_length() - 1, SEG_ALIGN.bit_length() - 2, -1):
        size = 1 << bit

        @pl.when((total & size) != 0)
        def _():
            make_copy(0, 0, size).wait()


def _dispatch_kernel(last_ref, nused_ref, total_ref, seg_ref, lpos_ref, x_ref, xs_hbm, sort_scr,
                     zero_scr, sem, zsem, *, tt, tm, n_tiles, n_sorted):
    @pl.when(pl.program_id(0) == 0)
    def _():
        zero_scr[...] = jnp.zeros(zero_scr.shape, U32)

        def zero_copy(row):
            return pltpu.make_async_copy(
                zero_scr, xs_hbm.at[pl.ds(pl.multiple_of(row, tm), tm), :], zsem)

        for e in range(N_EXPERTS):
            @pl.when(last_ref[e] >= 0)
            def _():
                zero_copy(last_ref[e]).start()

        def start_tail(i, carry):
            zero_copy(i * tm).start()
            return carry

        lax.fori_loop(nused_ref[0], n_tiles, start_tail, 0)

        for e in range(N_EXPERTS):
            @pl.when(last_ref[e] >= 0)
            def _():
                zero_copy(last_ref[e]).wait()

        def wait_tail(i, carry):
            zero_copy(i * tm).wait()
            return carry

        lax.fori_loop(nused_ref[0], n_tiles, wait_tail, 0)

    lp = lpos_ref[0]
    r_iota = lax.broadcasted_iota(I32, (n_sorted, tt), 0)
    perm = jnp.where(r_iota == lp[0:1, :], 1.0, 0.0)
    for k in range(1, TOP_K):
        perm = jnp.where(r_iota == lp[k:k + 1, :], 1.0, perm)
    perm = perm.astype(BF16)
    lo_bits = lax.bitcast_convert_type(_dot(perm, x_ref[:, 0:HALF]), U32)
    hi_bits = lax.bitcast_convert_type(_dot(perm, x_ref[:, HALF:]), U32)
    i = pl.program_id(0)
    buf = lax.rem(i, 2)
    sort_scr[buf] = (lo_bits >> 16) | (hi_bits & jnp.uint32(0xFFFF0000))

    def seg_copy_on(slot):
        def seg_copy(lo, dst, size):
            return pltpu.make_async_copy(sort_scr.at[slot, pl.ds(lo, size), :],
                                         xs_hbm.at[pl.ds(dst, size), :], sem.at[slot])
        return seg_copy

    _segment_starts(seg_ref, seg_copy_on(buf))

    @pl.when(i > 0)
    def _():
        _segment_waits(total_ref[i - 1], seg_copy_on(1 - buf), n_sorted=n_sorted)

    @pl.when(i == pl.num_programs(0) - 1)
    def _():
        _segment_waits(total_ref[i], seg_copy_on(buf), n_sorted=n_sorted)


def _dispatch(x2b, lpos_tiles, segs, last_tile_row, n_used, seg_totals, n_rows, *, tt, tm,
              n_sorted):
    n_tok = x2b.shape[0]
    grid_spec = pltpu.PrefetchScalarGridSpec(
        num_scalar_prefetch=3,
        grid=(n_tok // tt,),
        in_specs=[
            pl.BlockSpec((1, 1, SEG_WIDTH), lambda i, la, nu, to: (i, 0, 0),
                         memory_space=pltpu.SMEM),
            pl.BlockSpec((1, TOP_K, tt), lambda i, la, nu, to: (i, 0, 0)),
            pl.BlockSpec((tt, D_MODEL), lambda i, la, nu, to: (i, 0)),
        ],
        out_specs=pl.BlockSpec(memory_space=pl.ANY),
        scratch_shapes=[pltpu.VMEM((2, n_sorted, HALF), U32), pltpu.VMEM((tm, HALF), U32),
                        pltpu.SemaphoreType.DMA((2,)), pltpu.SemaphoreType.DMA],
    )
    return pl.pallas_call(
        functools.partial(_dispatch_kernel, tt=tt, tm=tm, n_tiles=n_rows // tm,
                          n_sorted=n_sorted),
        grid_spec=grid_spec,
        out_shape=jax.ShapeDtypeStruct((n_rows, HALF), U32),
        compiler_params=pltpu.CompilerParams(
            dimension_semantics=("arbitrary",), has_side_effects=True,
            vmem_limit_bytes=VMEM_LIMIT_BYTES),
        name="moe_dispatch",
    )(last_tile_row, n_used, seg_totals, segs, lpos_tiles, x2b)


def _ffn_kernel(te_ref, nused_ref, xs_ref, w1_ref, b1_ref, w2_ref, b2_ref, ys_ref, w1_scr, w2_scr):
    i = pl.program_id(0)
    active = i < nused_ref[0]
    switch = (i == 0) | (te_ref[i] != te_ref[jnp.maximum(i - 1, 0)])

    def ffn(w1g, w1u, w2):
        xt = _unpack_rows(xs_ref[...]).astype(BF16)
        gate = _dot(xt, w1g) + b1_ref[0, :, 0:D_FF]
        up = _dot(xt, w1u) + b1_ref[0, :, D_FF:]
        gate = jnp.minimum(gate, SWIGLU_LIMIT)
        up = jnp.clip(up, -SWIGLU_LIMIT, SWIGLU_LIMIT)
        hid = (up + 1.0) * gate * _sigmoid(SWIGLU_ALPHA * gate)
        ys_ref[...] = _pack_rows(_dot(hid.astype(BF16), w2) + b2_ref[0])

    @pl.when(active & switch)
    def _():
        w1g = w1_ref[0, :, 0:D_FF].astype(BF16)
        w1u = w1_ref[0, :, D_FF:].astype(BF16)
        w2 = w2_ref[0].astype(BF16)
        w1_scr[:, 0:D_FF] = w1g
        w1_scr[:, D_FF:] = w1u
        w2_scr[...] = w2
        ffn(w1g, w1u, w2)

    @pl.when(active & jnp.logical_not(switch))
    def _():
        ffn(w1_scr[:, 0:D_FF], w1_scr[:, D_FF:], w2_scr[...])

    @pl.when(i >= nused_ref[0])
    def _():
        ys_ref[...] = jnp.zeros(ys_ref.shape, U32)


def _grouped_ffn(xs, tile_expert, n_used, w1, b1, w2, b2, *, tm):
    n_rows = xs.shape[0]
    n_tiles = n_rows // tm
    row_blk = lambda i, te, nu: (jnp.minimum(i, nu[0] - 1), 0)
    exp_blk = lambda i, te, nu: (te[i], 0, 0)
    grid_spec = pltpu.PrefetchScalarGridSpec(
        num_scalar_prefetch=2,
        grid=(n_tiles,),
        in_specs=[
            pl.BlockSpec((tm, HALF), row_blk),
            pl.BlockSpec((1, D_MODEL, 2 * D_FF), exp_blk),
            pl.BlockSpec((1, 1, 2 * D_FF), exp_blk),
            pl.BlockSpec((1, D_FF, D_MODEL), exp_blk),
            pl.BlockSpec((1, 1, D_MODEL), exp_blk),
        ],
        out_specs=pl.BlockSpec((tm, HALF), lambda i, te, nu: (i, 0)),
        scratch_shapes=[pltpu.VMEM((D_MODEL, 2 * D_FF), BF16), pltpu.VMEM((D_FF, D_MODEL), BF16)],
    )
    return pl.pallas_call(
        _ffn_kernel,
        grid_spec=grid_spec,
        out_shape=jax.ShapeDtypeStruct((n_rows, HALF), U32),
        compiler_params=_cparams(1),
        name="moe_grouped_ffn",
    )(tile_expert, n_used, xs, w1, b1, w2, b2)


def _combine_kernel(seg_ref, segn_ref, x_ref, lpos_ref, tw_ref, g_ref, b_ref, ys_hbm, o_ref,
                    rows_scr, sem, *, tt, n_sorted):
    i = pl.program_id(0)
    buf = lax.rem(i, 2)

    def seg_copy_on(slot):
        def seg_copy(lo, src, size):
            return pltpu.make_async_copy(ys_hbm.at[pl.ds(src, size), :],
                                         rows_scr.at[slot, pl.ds(lo, size), :], sem.at[slot])
        return seg_copy

    @pl.when(i == 0)
    def _():
        rows_scr[...] = jnp.zeros(rows_scr.shape, U32)
        _segment_starts(seg_ref, seg_copy_on(0))

    @pl.when(i + 1 < pl.num_programs(0))
    def _():
        _segment_starts(segn_ref, seg_copy_on(1 - buf))

    lp = lpos_ref[...]
    tw = tw_ref[...]
    c_iota = lax.broadcasted_iota(I32, (WMAT_BLOCK, n_sorted), 1)
    blocks = []
    for t0 in range(0, tt, WMAT_BLOCK):
        lp0 = lp[t0:t0 + WMAT_BLOCK, :]
        tw0 = tw[t0:t0 + WMAT_BLOCK, :]
        blk = jnp.where(c_iota == lp0[:, 0:1], tw0[:, 0:1], 0.0)
        for k in range(1, TOP_K):
            blk = jnp.where(c_iota == lp0[:, k:k + 1], tw0[:, k:k + 1], blk)
        blocks.append(blk.astype(BF16))
    wmat = jnp.concatenate(blocks, axis=0)

    _segment_waits(seg_ref[0, 0, 3 * N_EXPERTS], seg_copy_on(buf), n_sorted=n_sorted)

    rows = rows_scr[buf]
    r_lo = lax.bitcast_convert_type(rows << 16, F32).astype(BF16)
    r_hi = lax.bitcast_convert_type(rows & jnp.uint32(0xFFFF0000), F32).astype(BF16)
    y = jnp.concatenate([_dot(wmat, r_lo), _dot(wmat, r_hi)], axis=1)
    o_ref[...] = _layer_norm(DN_ALPHA * x_ref[...] + y, g_ref[...], b_ref[...])


def _combine(x2, ys, segs, lpos_tok, tw_tok, g, b, *, tt, n_sorted):
    n_tok = x2.shape[0]
    const2 = lambda i: (0, 0)
    return pl.pallas_call(
        functools.partial(_combine_kernel, tt=tt, n_sorted=n_sorted),
        grid=(n_tok // tt,),
        in_specs=[
            pl.BlockSpec((1, 1, SEG_WIDTH), lambda i: (i, 0, 0), memory_space=pltpu.SMEM),
            pl.BlockSpec((1, 1, SEG_WIDTH), lambda i: (jnp.minimum(i + 1, n_tok // tt - 1), 0, 0),
                         memory_space=pltpu.SMEM),
            pl.BlockSpec((tt, D_MODEL), lambda i: (i, 0)),
            pl.BlockSpec((tt, TOP_K), lambda i: (i, 0)),
            pl.BlockSpec((tt, TOP_K), lambda i: (i, 0)),
            pl.BlockSpec((1, D_MODEL), const2),
            pl.BlockSpec((1, D_MODEL), const2),
            pl.BlockSpec(memory_space=pl.ANY),
        ],
        out_specs=pl.BlockSpec((tt, D_MODEL), lambda i: (i, 0)),
        out_shape=jax.ShapeDtypeStruct((n_tok, D_MODEL), F32),
        scratch_shapes=[pltpu.VMEM((2, n_sorted, HALF), U32), pltpu.SemaphoreType.DMA((2,))],
        compiler_params=_cparams(1),
        name="moe_combine",
    )(segs, segs, x2, lpos_tok, tw_tok, g.reshape(1, -1), b.reshape(1, -1), ys)


def _moe_layer(x2, x2b, tw, lpos, meta, counts, w1, b1, w2, b2, g, b, layer, *, tm, tt):
    bsz, s, _ = x2.shape
    n_tok = bsz * s
    n_tok_tiles = n_tok // tt
    seg_pad = N_EXPERTS * (SEG_ALIGN - 1)
    n_sorted = -(-(TOP_K * tt + seg_pad) // MXU_DIM) * MXU_DIM
    n_tiles = -(-(n_tok * TOP_K + n_tok_tiles * seg_pad) // tm) + N_EXPERTS
    n_rows = n_tiles * tm

    tiles_e = (counts + tm - 1) // tm
    tile_end = jnp.cumsum(tiles_e)
    start_e = (tile_end - tiles_e) * tm
    n_used = tile_end[-1]
    last_tile_row = jnp.where(tiles_e > 0, (tile_end - 1) * tm, -1).astype(I32)
    n_used_arr = n_used.reshape(1).astype(I32)
    tile_ids = jnp.minimum(jnp.arange(n_tiles, dtype=I32), n_used - 1)
    tile_expert = jnp.sum(tile_ids[:, None] >= tile_end[None, :], axis=1).astype(I32)
    seg_len, seg_lo = meta[:, :, 0], meta[:, :, 1]
    seg_total = jnp.broadcast_to(seg_lo[:, -1:] + seg_len[:, -1:], (n_tok_tiles, N_EXPERTS))
    segs = jnp.concatenate([seg_len, seg_lo, meta[:, :, 2] + start_e[None, :], seg_total], axis=1)
    segs = segs.reshape(n_tok_tiles, 1, SEG_WIDTH).astype(I32)
    lpos_tiles = lpos.reshape(bsz, TOP_K, s // tt, tt).transpose(0, 2, 1, 3)
    lpos_tiles = lpos_tiles.reshape(n_tok_tiles, TOP_K, tt)

    xs = _dispatch(x2b.reshape(n_tok, D_MODEL), lpos_tiles, segs, last_tile_row, n_used_arr,
                   seg_total[:, 0].astype(I32), n_rows, tt=tt, tm=tm, n_sorted=n_sorted)
    ys = _grouped_ffn(xs, tile_expert + layer * N_EXPERTS, n_used_arr, w1, b1, w2, b2, tm=tm)
    lpos_tok = lpos.transpose(0, 2, 1).reshape(n_tok, TOP_K)
    tw_tok = tw.transpose(0, 2, 1).reshape(n_tok, TOP_K)
    out = _combine(x2.reshape(n_tok, D_MODEL), ys, segs, lpos_tok, tw_tok, g, b, tt=tt,
                   n_sorted=n_sorted)
    return out.reshape(bsz, s, D_MODEL)


def _pick_tile(n, want):
    t = min(n, want)
    while n % t:
        t //= 2
    return t


def kernel(x, mem, ev_w_in, ev_conv_a, ev_conv_b, ev_conv_b_bias, ev_w_rgate, ev_b_rgate, ev_w_igate, ev_b_igate, ev_lambda, ev_w_out, od_w_in, od_b_gates, od_norm_g, od_w_out, xa_wq, xa_wk, xa_wv, xa_wo, moe_w_router, moe_b_router, moe_w1, moe_b1, moe_w2, moe_b2, ln_g, ln_b):
    bsz, s, _ = x.shape
    ts_even = _pick_tile(s, 512)
    ts_odd = _pick_tile(s, 256)
    ts_attn = _pick_tile(s, 512)
    n_sub_attn = 2 if s % (2 * ts_attn) == 0 else 1
    tm = 1024

    w1_all = moe_w1.reshape(DEPTH * N_EXPERTS, D_MODEL, 2 * D_FF)
    w2_all = moe_w2.reshape(DEPTH * N_EXPERTS, D_FF, D_MODEL)
    b1_all = moe_b1.reshape(DEPTH * N_EXPERTS, 1, 2 * D_FF)
    b2_all = moe_b2.reshape(DEPTH * N_EXPERTS, 1, D_MODEL)
    for layer in range(DEPTH):
        j = layer // 2
        if layer % 2 == 0:
            x = _even_layer(x, ev_w_in[j], ev_conv_a[j], ev_conv_b[j], ev_conv_b_bias[j],
                            ev_w_rgate[j], ev_b_rgate[j], ev_w_igate[j], ev_b_igate[j],
                            ev_lambda[j], ev_w_out[j], ln_g[layer, 0], ln_b[layer, 0], ts=ts_even)
        else:
            x = _odd_layer(x, od_w_in[j], od_b_gates[j], od_norm_g[j], od_w_out[j],
                           ln_g[layer, 0], ln_b[layer, 0], ts=ts_odd)
        qk_mem, vo_mem = _kv_proj(mem, xa_wk[layer], xa_wv[layer], xa_wq[layer], xa_wo[layer])
        x2, x2b, tw, lpos, meta, cnt = _attn_router(
            x, qk_mem, vo_mem, ln_g[layer, 1], ln_b[layer, 1],
            moe_w_router[layer], moe_b_router[layer], ts=ts_attn, n_sub=n_sub_attn)
        x = _moe_layer(x2, x2b, tw, lpos, meta, cnt[:, 0], w1_all, b1_all, w2_all, b2_all,
                       ln_g[layer, 2], ln_b[layer, 2], layer, tm=tm, tt=ts_attn)
    return x
```

```python
import functools

import jax
import jax.numpy as jnp
from jax import lax
from jax.experimental import pallas as pl
from jax.experimental.pallas import tpu as pltpu

F32 = jnp.float32
BF16 = jnp.bfloat16
U32 = jnp.uint32
I32 = jnp.int32

D_MODEL = 1024
DEPTH = 4
A_WIDTH = 512
B_WIDTH = 1024
B_HEADS = 8
B_HEAD_DIM = 128
LRU_C = 8.0
M_HEADS = 4
M_QK_DIM = 128
M_V_DIM = 256
M_QK = 512
M_V = 1024
M_CHUNK = 128
X_HEADS = 4
X_HEAD_DIM = 256
N_EXPERTS = 32
TOP_K = 4
D_FF = 1024
SWIGLU_LIMIT = 7.0
SWIGLU_ALPHA = 1.702
DN_ALPHA = (2 * DEPTH) ** 0.25
LN_EPS = 1e-5
RMS_EPS = 1e-6
HALF = D_MODEL // 2
SEG_ALIGN = 8
MXU_DIM = 256

VMEM_LIMIT_BYTES = 56 * 1024 * 1024


def _cparams(n_grid):
    return pltpu.CompilerParams(
        dimension_semantics=("arbitrary",) * n_grid, vmem_limit_bytes=VMEM_LIMIT_BYTES)


def _layer_norm(y, g, b):
    mu = jnp.mean(y, axis=-1, keepdims=True)
    yc = y - mu
    var = jnp.mean(yc * yc, axis=-1, keepdims=True)
    return yc * lax.rsqrt(var + LN_EPS) * g + b


def _sigmoid(x):
    return 1.0 / (1.0 + jnp.exp(-x))


def _softplus(x):
    return jnp.maximum(x, 0.0) + jnp.log1p(jnp.exp(-jnp.abs(x)))


def _gelu_tanh(x):
    return 0.5 * x * (1.0 + jnp.tanh(0.7978845608028654 * (x + 0.044715 * (x * x * x))))


def _split_bf16(x):
    hi = x.astype(BF16)
    lo = (x - hi.astype(F32)).astype(BF16)
    return hi, lo


def _dot(a, b):
    return jnp.dot(a, b, preferred_element_type=F32)


def _dot_nt(a, b):
    return lax.dot_general(a, b, (((1,), (1,)), ((), ())), preferred_element_type=F32)


def _dot_tn(a, b):
    return lax.dot_general(a, b, (((0,), (0,)), ((), ())), preferred_element_type=F32)


def _pack_rows(y):
    bits = lax.bitcast_convert_type(y.astype(BF16).astype(F32), U32)
    return (bits[:, :HALF] >> 16) | (bits[:, HALF:] & jnp.uint32(0xFFFF0000))


def _unpack_rows(w):
    lo = lax.bitcast_convert_type(w << 16, F32)
    hi = lax.bitcast_convert_type(w & jnp.uint32(0xFFFF0000), F32)
    return jnp.concatenate([lo, hi], axis=1)


def _even_kernel(x_ref, win_ref, ca_ref, cb_ref, cbb_ref, wr_ref, br_ref, wi_ref, bi_ref, lam_ref,
                 wout_ref, g_ref, b_ref, o_ref,
                 av_scr, bu_scr, a_scr, b_scr, h_scr, carry_scr, *, ts):
    @pl.when(pl.program_id(1) == 0)
    def _():
        av_scr[0:8, :] = jnp.zeros((8, A_WIDTH), F32)
        bu_scr[0:8, :] = jnp.zeros((8, B_WIDTH), F32)
        carry_scr[...] = jnp.zeros((8, B_WIDTH), F32)

    x = x_ref[0]
    z = _dot(x.astype(BF16), win_ref[...])
    a_b = z[:, 0:A_WIDTH]
    a_c = z[:, A_WIDTH:2 * A_WIDTH]
    a_x = z[:, 2 * A_WIDTH:3 * A_WIDTH]
    b_u = z[:, 3 * A_WIDTH:3 * A_WIDTH + B_WIDTH]
    b_g = z[:, 3 * A_WIDTH + B_WIDTH:]

    av_scr[8:8 + ts, :] = a_c * a_x
    ca = ca_ref[...]
    conv_a = (ca[2:3, :] * av_scr[8:8 + ts, :] + ca[1:2, :] * av_scr[7:7 + ts, :]
              + ca[0:1, :] * av_scr[6:6 + ts, :])
    y_a = a_b * conv_a
    av_scr[0:8, :] = av_scr[ts:ts + 8, :]

    bu_scr[8:8 + ts, :] = b_u
    cb = cb_ref[...]
    u = (cb[3:4, :] * bu_scr[8:8 + ts, :] + cb[2:3, :] * bu_scr[7:7 + ts, :]
         + cb[1:2, :] * bu_scr[6:6 + ts, :] + cb[0:1, :] * bu_scr[5:5 + ts, :] + cbb_ref[...])
    bu_scr[0:8, :] = bu_scr[ts:ts + 8, :]

    ub = u.astype(BF16)
    r_parts, i_parts = [], []
    for h in range(B_HEADS):
        uh = ub[:, h * B_HEAD_DIM:(h + 1) * B_HEAD_DIM]
        r_parts.append(_dot(uh, wr_ref[h]))
        i_parts.append(_dot(uh, wi_ref[h]))
    r = _sigmoid(jnp.concatenate(r_parts, axis=1) + br_ref[...])
    ig = _sigmoid(jnp.concatenate(i_parts, axis=1) + bi_ref[...])
    log_a = (-LRU_C) * r * _softplus(-lam_ref[...])
    a = jnp.exp(log_a)
    a_scr[...] = a
    b_scr[...] = jnp.sqrt(1.0 - a * a) * (ig * u)

    row8 = lax.broadcasted_iota(I32, (8, B_WIDTH), 0)

    def group(i, carry):
        r0 = pl.multiple_of(i * 8, 8)
        ga = a_scr[pl.ds(r0, 8), :]
        gb = b_scr[pl.ds(r0, 8), :]
        for d in (1, 2, 4):
            keep = row8 >= d
            a_sh = jnp.where(keep, pltpu.roll(ga, d, 0), 1.0)
            b_sh = jnp.where(keep, pltpu.roll(gb, d, 0), 0.0)
            gb = ga * b_sh + gb
            ga = ga * a_sh
        hg = gb + ga * carry
        h_scr[pl.ds(r0, 8), :] = hg
        return jnp.broadcast_to(hg[7:8, :], (8, B_WIDTH))

    carry_scr[...] = lax.fori_loop(0, ts // 8, group, carry_scr[...], unroll=4)

    y_b = _gelu_tanh(b_g) * h_scr[...]
    mix = (_dot(y_a.astype(BF16), wout_ref[0:A_WIDTH, :])
           + _dot(y_b.astype(BF16), wout_ref[A_WIDTH:, :]))
    o_ref[0] = _layer_norm(DN_ALPHA * x + mix, g_ref[...], b_ref[...])


def _even_layer(x, w_in, conv_a, conv_b, conv_b_bias, w_r, b_r, w_i, b_i, lam, w_out, g, b, *, ts):
    bsz, s, _ = x.shape
    ev_in = 3 * A_WIDTH + 2 * B_WIDTH
    const2 = lambda bi, si: (0, 0)
    const3 = lambda bi, si: (0, 0, 0)
    return pl.pallas_call(
        functools.partial(_even_kernel, ts=ts),
        grid=(bsz, s // ts),
        in_specs=[
            pl.BlockSpec((1, ts, D_MODEL), lambda bi, si: (bi, si, 0)),
            pl.BlockSpec((D_MODEL, ev_in), const2),
            pl.BlockSpec((3, A_WIDTH), const2),
            pl.BlockSpec((4, B_WIDTH), const2),
            pl.BlockSpec((1, B_WIDTH), const2),
            pl.BlockSpec((B_HEADS, B_HEAD_DIM, B_HEAD_DIM), const3),
            pl.BlockSpec((1, B_WIDTH), const2),
            pl.BlockSpec((B_HEADS, B_HEAD_DIM, B_HEAD_DIM), const3),
            pl.BlockSpec((1, B_WIDTH), const2),
            pl.BlockSpec((1, B_WIDTH), const2),
            pl.BlockSpec((A_WIDTH + B_WIDTH, D_MODEL), const2),
            pl.BlockSpec((1, D_MODEL), const2),
            pl.BlockSpec((1, D_MODEL), const2),
        ],
        out_specs=pl.BlockSpec((1, ts, D_MODEL), lambda bi, si: (bi, si, 0)),
        out_shape=jax.ShapeDtypeStruct(x.shape, F32),
        scratch_shapes=[
            pltpu.VMEM((ts + 8, A_WIDTH), F32),
            pltpu.VMEM((ts + 8, B_WIDTH), F32),
            pltpu.VMEM((ts, B_WIDTH), F32),
            pltpu.VMEM((ts, B_WIDTH), F32),
            pltpu.VMEM((ts, B_WIDTH), F32),
            pltpu.VMEM((8, B_WIDTH), F32),
        ],
        compiler_params=_cparams(2),
        name="even_mixer",
    )(x, w_in.astype(BF16), conv_a, conv_b, conv_b_bias.reshape(1, -1), w_r.astype(BF16),
      b_r.reshape(1, -1), w_i.astype(BF16), b_i.reshape(1, -1), lam.reshape(1, -1),
      w_out.astype(BF16), g.reshape(1, -1), b.reshape(1, -1))


GATE_PAD = 128
V_EXT = M_V_DIM + 128


def _odd_kernel(x_ref, win_ref, wg_ref, bgc_ref, ng_ref, wout_ref, g_ref, b_ref,
                o_ref, c_scr, m_scr, h_scr, *, ts):
    L = M_CHUNK

    @pl.when(pl.program_id(1) == 0)
    def _():
        c_scr[...] = jnp.zeros(c_scr.shape, F32)
        m_scr[...] = jnp.zeros(m_scr.shape, F32)

    x = x_ref[0]
    x_hi, x_lo = _split_bf16(x)
    z = _dot(x_hi, win_ref[...])
    q_all = (z[:, 0:M_QK] * (M_QK_DIM ** -0.5)).astype(BF16)
    k_all = z[:, M_QK:2 * M_QK].astype(BF16)
    v_all = z[:, 2 * M_QK:2 * M_QK + M_V].astype(BF16)
    o_all = z[:, 2 * M_QK + M_V:]

    wg_hi, wg_lo = _split_bf16(wg_ref[...])
    both = _dot(x_hi, jnp.concatenate([wg_hi, wg_lo], axis=1))
    gates_c = both[:, 0:GATE_PAD] + _dot(x_lo, wg_hi) + both[:, GATE_PAD:] + bgc_ref[...]
    logf_c = -_softplus(-gates_c)

    rows = lax.broadcasted_iota(I32, (L, L), 0)
    cols = lax.broadcasted_iota(I32, (L, L), 1)
    causal = rows >= cols
    tril = jnp.where(causal, 1.0, 0.0).astype(BF16)
    lane0 = jnp.where(lax.broadcasted_iota(I32, (L, 128), 1) == 0, 1.0, 0.0).astype(BF16)

    for c in range(ts // L):
        sl = slice(c * L, (c + 1) * L)
        fc_hi, fc_lo = _split_bf16(logf_c[sl, :])
        fc_lo2 = (logf_c[sl, :] - fc_hi.astype(F32) - fc_lo.astype(F32)).astype(BF16)
        bcum_c = _dot(tril, fc_hi) + _dot(tril, fc_lo) + _dot(tril, fc_lo2)
        i_minus_b = gates_c[sl, :] - pltpu.roll(bcum_c, GATE_PAD - M_HEADS, 1)
        i_minus_b_t = i_minus_b.T
        for h in range(M_HEADS):
            m_st = m_scr[h][0:1, 0:1]
            b_col = bcum_c[:, M_HEADS + h:M_HEADS + h + 1]
            i_col = gates_c[sl, h:h + 1]
            b_end = b_col[L - 1:L, :]

            d = jnp.where(causal, b_col + i_minus_b_t[h:h + 1, :], -jnp.inf)
            inter = b_col + m_st
            m_t = jnp.maximum(inter, jnp.max(d, axis=1, keepdims=True))
            p = jnp.exp(d - m_t)
            w_inter = jnp.exp(inter - m_t)

            qh = q_all[sl, h * M_QK_DIM:(h + 1) * M_QK_DIM]
            kh = k_all[sl, h * M_QK_DIM:(h + 1) * M_QK_DIM]
            v_ext = jnp.concatenate([v_all[sl, h * M_V_DIM:(h + 1) * M_V_DIM], lane0], axis=1)
            qk = (_dot_nt(qh, kh) * p).astype(BF16)
            c_ext = c_scr[h]
            nd = _dot(qk, v_ext) + w_inter * _dot(qh, c_ext.astype(BF16))
            den = nd[:, M_V_DIM:M_V_DIM + 1]
            hh = nd[:, 0:M_V_DIM] / jnp.maximum(jnp.abs(den), jnp.exp(-m_t))

            g_col = b_end - b_col + i_col
            m_new = jnp.maximum(b_end + m_st, jnp.max(g_col, axis=0, keepdims=True))
            wg_col = jnp.exp(g_col - m_new)
            decay = jnp.exp(b_end + m_st - m_new)
            kv = _dot_tn(kh, (wg_col * v_ext.astype(F32)).astype(BF16))
            c_scr[h] = decay * c_ext + kv
            m_scr[h] = jnp.broadcast_to(m_new, (8, 128))

            hn = hh * lax.rsqrt(jnp.mean(hh * hh, axis=-1, keepdims=True) + RMS_EPS)
            h_scr[sl, h * M_V_DIM:(h + 1) * M_V_DIM] = hn

    gated = _sigmoid(o_all) * (h_scr[...] * ng_ref[...])
    mix = _dot(gated.astype(BF16), wout_ref[...])
    o_ref[0] = _layer_norm(DN_ALPHA * x + mix, g_ref[...], b_ref[...])


def _odd_layer(x, w_in, b_gates, norm_g, w_out, g, b, *, ts):
    bsz, s, _ = x.shape
    n_main = 2 * M_QK + 2 * M_V
    w_main = w_in[:, :n_main].astype(BF16)
    w_gate = w_in[:, n_main:]
    wg_pad = jnp.pad(w_gate, ((0, 0), (0, GATE_PAD - 2 * M_HEADS)))
    bg_col = jnp.pad(b_gates, (0, GATE_PAD - 2 * M_HEADS)).reshape(1, GATE_PAD)
    const2 = lambda bi, si: (0, 0)
    return pl.pallas_call(
        functools.partial(_odd_kernel, ts=ts),
        grid=(bsz, s // ts),
        in_specs=[
            pl.BlockSpec((1, ts, D_MODEL), lambda bi, si: (bi, si, 0)),
            pl.BlockSpec((D_MODEL, n_main), const2),
            pl.BlockSpec((D_MODEL, GATE_PAD), const2),
            pl.BlockSpec((1, GATE_PAD), const2),
            pl.BlockSpec((1, M_V), const2),
            pl.BlockSpec((M_V, D_MODEL), const2),
            pl.BlockSpec((1, D_MODEL), const2),
            pl.BlockSpec((1, D_MODEL), const2),
        ],
        out_specs=pl.BlockSpec((1, ts, D_MODEL), lambda bi, si: (bi, si, 0)),
        out_shape=jax.ShapeDtypeStruct(x.shape, F32),
        scratch_shapes=[
            pltpu.VMEM((M_HEADS, M_QK_DIM, V_EXT), F32),
            pltpu.VMEM((M_HEADS, 8, 128), F32),
            pltpu.VMEM((ts, M_V), F32),
        ],
        compiler_params=_cparams(2),
        name="odd_mixer",
    )(x, w_main, wg_pad, bg_col, norm_g.reshape(1, -1), w_out.astype(BF16),
      g.reshape(1, -1), b.reshape(1, -1))


def _kv_kernel(mem_ref, wk_ref, wv_ref, wq_ref, wo_ref, qk_ref, vo_ref):
    n_mem = mem_ref.shape[1]
    m = mem_ref[0].astype(BF16)
    k = _dot(m, wk_ref[...]).astype(BF16)
    v = _dot(m, wv_ref[...]).astype(BF16)
    for h in range(X_HEADS):
        hs = slice(h * X_HEAD_DIM, (h + 1) * X_HEAD_DIM)
        ms = slice(h * n_mem, (h + 1) * n_mem)
        qk_ref[0, :, ms] = (_dot_nt(wq_ref[:, hs], k[:, hs]) * (X_HEAD_DIM ** -0.5)).astype(BF16)
        vo_ref[0, ms, :] = _dot(v[:, hs], wo_ref[hs, :]).astype(BF16)


def _kv_proj(mem, wk, wv, wq, wo):
    bsz, n_mem, _ = mem.shape
    const2 = lambda bi: (0, 0)
    wblk = pl.BlockSpec((D_MODEL, D_MODEL), const2)
    return pl.pallas_call(
        _kv_kernel,
        grid=(bsz,),
        in_specs=[pl.BlockSpec((1, n_mem, D_MODEL), lambda bi: (bi, 0, 0)), wblk, wblk, wblk, wblk],
        out_specs=[pl.BlockSpec((1, D_MODEL, X_HEADS * n_mem), lambda bi: (bi, 0, 0)),
                   pl.BlockSpec((1, X_HEADS * n_mem, D_MODEL), lambda bi: (bi, 0, 0))],
        out_shape=[jax.ShapeDtypeStruct((bsz, D_MODEL, X_HEADS * n_mem), BF16),
                   jax.ShapeDtypeStruct((bsz, X_HEADS * n_mem, D_MODEL), BF16)],
        compiler_params=_cparams(1),
        name="memory_kv",
    )(mem, wk.astype(BF16), wv.astype(BF16), wq.astype(BF16), wo.astype(BF16))


def _attn_tile(x, qk, vo, g_ref, b_ref, wrt_ref, brt_ref, taken):
    ts = x.shape[0]
    n_mem = qk.shape[1] // X_HEADS
    sc_all = _dot(x.astype(BF16), qk)
    probs = []
    for h in range(X_HEADS):
        sc = sc_all[:, h * n_mem:(h + 1) * n_mem]
        e = jnp.exp(sc - jnp.max(sc, axis=-1, keepdims=True))
        probs.append((e / jnp.sum(e, axis=-1, keepdims=True)).astype(BF16))
    att = _dot(jnp.concatenate(probs, axis=1), vo)
    x2 = _layer_norm(DN_ALPHA * x + att, g_ref[...], b_ref[...])

    x_hi, x_lo = _split_bf16(x2)
    w_hi, w_lo = _split_bf16(wrt_ref[...])
    both = _dot_nt(jnp.concatenate([w_hi, w_lo], axis=0), x_hi)
    logits = both[0:N_EXPERTS, :] + _dot_nt(w_hi, x_lo) + both[N_EXPERTS:, :] + brt_ref[...]
    e_iota = lax.broadcasted_iota(I32, (N_EXPERTS, ts), 0)
    onehots, vals = [], []
    for _ in range(TOP_K):
        mx = jnp.max(logits, axis=0, keepdims=True)
        sel = jnp.min(jnp.where(logits == mx, e_iota, N_EXPERTS), axis=0, keepdims=True)
        oh = e_iota == sel
        onehots.append(oh)
        vals.append(mx)
        logits = jnp.where(oh, -jnp.inf, logits)
    exps = [jnp.exp(val - vals[0]) for val in vals]
    tot = exps[0] + exps[1] + exps[2] + exps[3]
    tw = jnp.concatenate([ex / tot for ex in exps], axis=0)

    member = jnp.where(onehots[0] | onehots[1] | onehots[2] | onehots[3], 1.0, 0.0)
    t_r = lax.broadcasted_iota(I32, (ts, ts), 0)
    t_c = lax.broadcasted_iota(I32, (ts, ts), 1)
    before = jnp.where(t_r < t_c, 1.0, 0.0).astype(BF16)
    pos = _dot(member.astype(BF16), before)
    n_e = jnp.sum(member, axis=1, keepdims=True)
    q_e = jnp.floor((n_e + (SEG_ALIGN - 1.0)) * (1.0 / SEG_ALIGN))
    e_r = lax.broadcasted_iota(I32, (N_EXPERTS, N_EXPERTS), 0)
    e_c = lax.broadcasted_iota(I32, (N_EXPERTS, N_EXPERTS), 1)
    lower = jnp.where(e_c < e_r, 1.0, 0.0).astype(BF16)
    q_wide = jnp.broadcast_to(q_e, (N_EXPERTS, 128)).astype(BF16)
    lo_e = SEG_ALIGN * _dot(lower, q_wide)[:, 0:1]
    p_e = SEG_ALIGN * q_e
    base = lo_e + pos
    lpos = [jnp.sum(jnp.where(oh, base, 0.0), axis=0, keepdims=True) for oh in onehots]
    lpos = jnp.concatenate(lpos, axis=0).astype(I32)
    lane = lax.broadcasted_iota(I32, (N_EXPERTS, 128), 1)
    meta = jnp.where(lane == 0, p_e, jnp.where(lane == 1, lo_e, jnp.where(lane == 2, taken, 0.0)))
    return x2, x_hi, tw, lpos, meta.astype(I32), taken + p_e


def _attn_kernel(x_ref, qk_ref, vo_ref, g_ref, b_ref, wrt_ref, brt_ref,
                 x2_ref, x2b_ref, tw_ref, lpos_ref, meta_ref, cnt_ref, cnt_scr, *, ts, n_sub):
    @pl.when((pl.program_id(0) == 0) & (pl.program_id(1) == 0))
    def _():
        cnt_scr[...] = jnp.zeros(cnt_scr.shape, F32)

    taken = cnt_scr[...]
    for u in range(n_sub):
        rs = slice(u * ts, (u + 1) * ts)
        x2, x_hi, tw, lpos, meta, taken = _attn_tile(
            x_ref[0, rs, :], qk_ref[0], vo_ref[0], g_ref, b_ref, wrt_ref, brt_ref, taken)
        x2_ref[0, rs, :] = x2
        x2b_ref[0, rs, :] = x_hi
        tw_ref[0, :, rs] = tw
        lpos_ref[0, :, rs] = lpos
        meta_ref[u] = meta
    cnt_scr[...] = taken
    cnt_ref[...] = taken.astype(I32)


def _attn_router(x, qk_mem, vo_mem, g, b, w_router, b_router, *, ts, n_sub):
    bsz, s, _ = x.shape
    n_score = qk_mem.shape[2]
    tg = ts * n_sub
    const2 = lambda bi, si: (0, 0)
    tok = lambda bi, si: (bi, si, 0)
    lanes = lambda bi, si: (bi, 0, si)
    return pl.pallas_call(
        functools.partial(_attn_kernel, ts=ts, n_sub=n_sub),
        grid=(bsz, s // tg),
        in_specs=[
            pl.BlockSpec((1, tg, D_MODEL), tok),
            pl.BlockSpec((1, D_MODEL, n_score), lambda bi, si: (bi, 0, 0)),
            pl.BlockSpec((1, n_score, D_MODEL), lambda bi, si: (bi, 0, 0)),
            pl.BlockSpec((1, D_MODEL), const2),
            pl.BlockSpec((1, D_MODEL), const2),
            pl.BlockSpec((N_EXPERTS, D_MODEL), const2),
            pl.BlockSpec((N_EXPERTS, 1), const2),
        ],
        out_specs=[
            pl.BlockSpec((1, tg, D_MODEL), tok),
            pl.BlockSpec((1, tg, D_MODEL), tok),
            pl.BlockSpec((1, TOP_K, tg), lanes),
            pl.BlockSpec((1, TOP_K, tg), lanes),
            pl.BlockSpec((n_sub, N_EXPERTS, 128), lambda bi, si: (bi * (s // tg) + si, 0, 0)),
            pl.BlockSpec((N_EXPERTS, 128), const2),
        ],
        out_shape=[
            jax.ShapeDtypeStruct((bsz, s, D_MODEL), F32),
            jax.ShapeDtypeStruct((bsz, s, D_MODEL), BF16),
            jax.ShapeDtypeStruct((bsz, TOP_K, s), F32),
            jax.ShapeDtypeStruct((bsz, TOP_K, s), I32),
            jax.ShapeDtypeStruct((bsz * (s // ts), N_EXPERTS, 128), I32),
            jax.ShapeDtypeStruct((N_EXPERTS, 128), I32),
        ],
        scratch_shapes=[pltpu.VMEM((N_EXPERTS, 128), F32)],
        compiler_params=_cparams(2),
        name="memory_attention_router",
    )(x, qk_mem, vo_mem, g.reshape(1, -1), b.reshape(1, -1), w_router.T, b_router.reshape(-1, 1))


WMAT_BLOCK = 16
SEG_CHUNK = 128
SEG_WIDTH = 4 * N_EXPERTS


def _segment_starts(seg_ref, make_copy):
    chunk_bit = SEG_CHUNK.bit_length() - 1

    def per_expert(e, carry):
        n = seg_ref[0, 0, e]
        lo = seg_ref[0, 0, N_EXPERTS + e]
        dst = seg_ref[0, 0, 2 * N_EXPERTS + e]

        def chunk(j, c):
            off = pl.multiple_of(j * SEG_CHUNK, SEG_CHUNK)
            make_copy(pl.multiple_of(lo + off, SEG_ALIGN), pl.multiple_of(dst + off, SEG_ALIGN),
                      SEG_CHUNK).start()
            return c

        lax.fori_loop(0, n >> chunk_bit, chunk, 0)
        for bit in range(chunk_bit - 1, SEG_ALIGN.bit_length() - 2, -1):
            size = 1 << bit
            done = (n >> (bit + 1)) << (bit + 1)

            @pl.when((n & size) != 0)
            def _():
                make_copy(pl.multiple_of(lo + done, SEG_ALIGN),
                          pl.multiple_of(dst + done, SEG_ALIGN), size).start()
        return carry

    lax.fori_loop(0, N_EXPERTS, per_expert, 0)


def _segment_waits(total, make_copy, *, n_sorted):
    for bit in range(n_sorted.bit_length() - 1, SEG_ALIGN.bit_length() - 2, -1):
        size = 1 << bit

        @pl.when((total & size) != 0)
        def _():
            make_copy(0, 0, size).wait()


def _dispatch_kernel(last_ref, nused_ref, total_ref, seg_ref, lpos_ref, x_ref, xs_hbm, sort_scr,
                     zero_scr, sem, zsem, *, tt, tm, n_tiles, n_sorted):
    @pl.when(pl.program_id(0) == 0)
    def _():
        zero_scr[...] = jnp.zeros(zero_scr.shape, U32)

        def zero_copy(row):
            return pltpu.make_async_copy(
                zero_scr, xs_hbm.at[pl.ds(pl.multiple_of(row, tm), tm), :], zsem)

        for e in range(N_EXPERTS):
            @pl.when(last_ref[e] >= 0)
            def _():
                zero_copy(last_ref[e]).start()

        def start_tail(i, carry):
            zero_copy(i * tm).start()
            return carry

        lax.fori_loop(nused_ref[0], n_tiles, start_tail, 0)

        for e in range(N_EXPERTS):
            @pl.when(last_ref[e] >= 0)
            def _():
                zero_copy(last_ref[e]).wait()

        def wait_tail(i, carry):
            zero_copy(i * tm).wait()
            return carry

        lax.fori_loop(nused_ref[0], n_tiles, wait_tail, 0)

    lp = lpos_ref[0]
    lp16 = lp.astype(jnp.int16)
    r_iota = lax.broadcasted_iota(jnp.int16, (n_sorted, tt), 0)
    one = jnp.ones((), BF16)
    perm = jnp.where(r_iota == lp16[0:1, :], one, jnp.zeros((), BF16))
    for k in range(1, TOP_K):
        perm = jnp.where(r_iota == lp16[k:k + 1, :], one, perm)
    lo_bits = lax.bitcast_convert_type(_dot(perm, x_ref[:, 0:HALF]), U32)
    hi_bits = lax.bitcast_convert_type(_dot(perm, x_ref[:, HALF:]), U32)
    i = pl.program_id(0)
    buf = lax.rem(i, 2)
    sort_scr[buf] = (lo_bits >> 16) | (hi_bits & jnp.uint32(0xFFFF0000))

    def seg_copy_on(slot):
        def seg_copy(lo, dst, size):
            return pltpu.make_async_copy(sort_scr.at[slot, pl.ds(lo, size), :],
                                         xs_hbm.at[pl.ds(dst, size), :], sem.at[slot])
        return seg_copy

    _segment_starts(seg_ref, seg_copy_on(buf))

    @pl.when(i > 0)
    def _():
        _segment_waits(total_ref[i - 1], seg_copy_on(1 - buf), n_sorted=n_sorted)

    @pl.when(i == pl.num_programs(0) - 1)
    def _():
        _segment_waits(total_ref[i], seg_copy_on(buf), n_sorted=n_sorted)


def _dispatch(x2b, lpos_tiles, segs, last_tile_row, n_used, seg_totals, n_rows, *, tt, tm,
              n_sorted):
    n_tok = x2b.shape[0]
    grid_spec = pltpu.PrefetchScalarGridSpec(
        num_scalar_prefetch=3,
        grid=(n_tok // tt,),
        in_specs=[
            pl.BlockSpec((1, 1, SEG_WIDTH), lambda i, la, nu, to: (i, 0, 0),
                         memory_space=pltpu.SMEM),
            pl.BlockSpec((1, TOP_K, tt), lambda i, la, nu, to: (i, 0, 0)),
            pl.BlockSpec((tt, D_MODEL), lambda i, la, nu, to: (i, 0)),
        ],
        out_specs=pl.BlockSpec(memory_space=pl.ANY),
        scratch_shapes=[pltpu.VMEM((2, n_sorted, HALF), U32), pltpu.VMEM((tm, HALF), U32),
                        pltpu.SemaphoreType.DMA((2,)), pltpu.SemaphoreType.DMA],
    )
    return pl.pallas_call(
        functools.partial(_dispatch_kernel, tt=tt, tm=tm, n_tiles=n_rows // tm,
                          n_sorted=n_sorted),
        grid_spec=grid_spec,
        out_shape=jax.ShapeDtypeStruct((n_rows, HALF), U32),
        compiler_params=pltpu.CompilerParams(
            dimension_semantics=("arbitrary",), has_side_effects=True,
            vmem_limit_bytes=VMEM_LIMIT_BYTES),
        name="moe_dispatch",
    )(last_tile_row, n_used, seg_totals, segs, lpos_tiles, x2b)


def _ffn_kernel(te_ref, nused_ref, xs_ref, w1_ref, b1_ref, w2_ref, b2_ref, ys_ref, w1_scr, w2_scr):
    i = pl.program_id(0)
    active = i < nused_ref[0]
    switch = (i == 0) | (te_ref[i] != te_ref[jnp.maximum(i - 1, 0)])

    def ffn(w1g, w1u, w2):
        xt = _unpack_rows(xs_ref[...]).astype(BF16)
        gate = _dot(xt, w1g) + b1_ref[0, :, 0:D_FF]
        up = _dot(xt, w1u) + b1_ref[0, :, D_FF:]
        gate = jnp.minimum(gate, SWIGLU_LIMIT)
        up = jnp.clip(up, -SWIGLU_LIMIT, SWIGLU_LIMIT)
        hid = (up + 1.0) * gate * _sigmoid(SWIGLU_ALPHA * gate)
        ys_ref[...] = _pack_rows(_dot(hid.astype(BF16), w2) + b2_ref[0])

    @pl.when(active & switch)
    def _():
        w1g = w1_ref[0, :, 0:D_FF].astype(BF16)
        w1u = w1_ref[0, :, D_FF:].astype(BF16)
        w2 = w2_ref[0].astype(BF16)
        w1_scr[:, 0:D_FF] = w1g
        w1_scr[:, D_FF:] = w1u
        w2_scr[...] = w2
        ffn(w1g, w1u, w2)

    @pl.when(active & jnp.logical_not(switch))
    def _():
        ffn(w1_scr[:, 0:D_FF], w1_scr[:, D_FF:], w2_scr[...])

    @pl.when(i >= nused_ref[0])
    def _():
        ys_ref[...] = jnp.zeros(ys_ref.shape, U32)


def _grouped_ffn(xs, tile_expert, n_used, w1, b1, w2, b2, *, tm):
    n_rows = xs.shape[0]
    n_tiles = n_rows // tm
    row_blk = lambda i, te, nu: (jnp.minimum(i, nu[0] - 1), 0)
    exp_blk = lambda i, te, nu: (te[i], 0, 0)
    grid_spec = pltpu.PrefetchScalarGridSpec(
        num_scalar_prefetch=2,
        grid=(n_tiles,),
        in_specs=[
            pl.BlockSpec((tm, HALF), row_blk),
            pl.BlockSpec((1, D_MODEL, 2 * D_FF), exp_blk),
            pl.BlockSpec((1, 1, 2 * D_FF), exp_blk),
            pl.BlockSpec((1, D_FF, D_MODEL), exp_blk),
            pl.BlockSpec((1, 1, D_MODEL), exp_blk),
        ],
        out_specs=pl.BlockSpec((tm, HALF), lambda i, te, nu: (i, 0)),
        scratch_shapes=[pltpu.VMEM((D_MODEL, 2 * D_FF), BF16), pltpu.VMEM((D_FF, D_MODEL), BF16)],
    )
    return pl.pallas_call(
        _ffn_kernel,
        grid_spec=grid_spec,
        out_shape=jax.ShapeDtypeStruct((n_rows, HALF), U32),
        compiler_params=_cparams(1),
        name="moe_grouped_ffn",
    )(tile_expert, n_used, xs, w1, b1, w2, b2)


def _combine_kernel(seg_ref, segn_ref, x_ref, lpos_ref, tw_ref, g_ref, b_ref, ys_hbm, o_ref,
                    rows_scr, sem, *, tt, n_sorted):
    i = pl.program_id(0)
    buf = lax.rem(i, 2)

    def seg_copy_on(slot):
        def seg_copy(lo, src, size):
            return pltpu.make_async_copy(ys_hbm.at[pl.ds(src, size), :],
                                         rows_scr.at[slot, pl.ds(lo, size), :], sem.at[slot])
        return seg_copy

    @pl.when(i == 0)
    def _():
        rows_scr[...] = jnp.zeros(rows_scr.shape, U32)
        _segment_starts(seg_ref, seg_copy_on(0))

    @pl.when(i + 1 < pl.num_programs(0))
    def _():
        _segment_starts(segn_ref, seg_copy_on(1 - buf))

    lp = lpos_ref[...].astype(jnp.int16)
    tw = tw_ref[...].astype(BF16)
    c_iota = lax.broadcasted_iota(jnp.int16, (WMAT_BLOCK, n_sorted), 1)
    blocks = []
    for t0 in range(0, tt, WMAT_BLOCK):
        lp0 = lp[t0:t0 + WMAT_BLOCK, :]
        tw0 = tw[t0:t0 + WMAT_BLOCK, :]
        blk = jnp.where(c_iota == lp0[:, 0:1], tw0[:, 0:1], jnp.zeros((), BF16))
        for k in range(1, TOP_K):
            blk = jnp.where(c_iota == lp0[:, k:k + 1], tw0[:, k:k + 1], blk)
        blocks.append(blk)
    wmat = jnp.concatenate(blocks, axis=0)

    _segment_waits(seg_ref[0, 0, 3 * N_EXPERTS], seg_copy_on(buf), n_sorted=n_sorted)

    rows = rows_scr[buf]
    r_lo = lax.bitcast_convert_type(rows << 16, F32).astype(BF16)
    r_hi = lax.bitcast_convert_type(rows & jnp.uint32(0xFFFF0000), F32).astype(BF16)
    y = jnp.concatenate([_dot(wmat, r_lo), _dot(wmat, r_hi)], axis=1)
    o_ref[...] = _layer_norm(DN_ALPHA * x_ref[...] + y, g_ref[...], b_ref[...])


def _combine(x2, ys, segs, lpos_tok, tw_tok, g, b, *, tt, n_sorted):
    n_tok = x2.shape[0]
    const2 = lambda i: (0, 0)
    return pl.pallas_call(
        functools.partial(_combine_kernel, tt=tt, n_sorted=n_sorted),
        grid=(n_tok // tt,),
        in_specs=[
            pl.BlockSpec((1, 1, SEG_WIDTH), lambda i: (i, 0, 0), memory_space=pltpu.SMEM),
            pl.BlockSpec((1, 1, SEG_WIDTH), lambda i: (jnp.minimum(i + 1, n_tok // tt - 1), 0, 0),
                         memory_space=pltpu.SMEM),
            pl.BlockSpec((tt, D_MODEL), lambda i: (i, 0)),
            pl.BlockSpec((tt, TOP_K), lambda i: (i, 0)),
            pl.BlockSpec((tt, TOP_K), lambda i: (i, 0)),
            pl.BlockSpec((1, D_MODEL), const2),
            pl.BlockSpec((1, D_MODEL), const2),
            pl.BlockSpec(memory_space=pl.ANY),
        ],
        out_specs=pl.BlockSpec((tt, D_MODEL), lambda i: (i, 0)),
        out_shape=jax.ShapeDtypeStruct((n_tok, D_MODEL), F32),
        scratch_shapes=[pltpu.VMEM((2, n_sorted, HALF), U32), pltpu.SemaphoreType.DMA((2,))],
        compiler_params=_cparams(1),
        name="moe_combine",
    )(segs, segs, x2, lpos_tok, tw_tok, g.reshape(1, -1), b.reshape(1, -1), ys)


def _moe_layer(x2, x2b, tw, lpos, meta, counts, w1, b1, w2, b2, g, b, layer, *, tm, tt):
    bsz, s, _ = x2.shape
    n_tok = bsz * s
    n_tok_tiles = n_tok // tt
    seg_pad = N_EXPERTS * (SEG_ALIGN - 1)
    n_sorted = -(-(TOP_K * tt + seg_pad) // MXU_DIM) * MXU_DIM
    n_tiles = -(-(n_tok * TOP_K + n_tok_tiles * seg_pad) // tm) + N_EXPERTS
    n_rows = n_tiles * tm

    tiles_e = (counts + tm - 1) // tm
    tile_end = jnp.cumsum(tiles_e)
    start_e = (tile_end - tiles_e) * tm
    n_used = tile_end[-1]
    last_tile_row = jnp.where(tiles_e > 0, (tile_end - 1) * tm, -1).astype(I32)
    n_used_arr = n_used.reshape(1).astype(I32)
    tile_ids = jnp.minimum(jnp.arange(n_tiles, dtype=I32), n_used - 1)
    tile_expert = jnp.sum(tile_ids[:, None] >= tile_end[None, :], axis=1).astype(I32)
    seg_len, seg_lo = meta[:, :, 0], meta[:, :, 1]
    seg_total = jnp.broadcast_to(seg_lo[:, -1:] + seg_len[:, -1:], (n_tok_tiles, N_EXPERTS))
    segs = jnp.concatenate([seg_len, seg_lo, meta[:, :, 2] + start_e[None, :], seg_total], axis=1)
    segs = segs.reshape(n_tok_tiles, 1, SEG_WIDTH).astype(I32)
    lpos_tiles = lpos.reshape(bsz, TOP_K, s // tt, tt).transpose(0, 2, 1, 3)
    lpos_tiles = lpos_tiles.reshape(n_tok_tiles, TOP_K, tt)

    xs = _dispatch(x2b.reshape(n_tok, D_MODEL), lpos_tiles, segs, last_tile_row, n_used_arr,
                   seg_total[:, 0].astype(I32), n_rows, tt=tt, tm=tm, n_sorted=n_sorted)
    ys = _grouped_ffn(xs, tile_expert + layer * N_EXPERTS, n_used_arr, w1, b1, w2, b2, tm=tm)
    lpos_tok = lpos.transpose(0, 2, 1).reshape(n_tok, TOP_K)
    tw_tok = tw.transpose(0, 2, 1).reshape(n_tok, TOP_K)
    out = _combine(x2.reshape(n_tok, D_MODEL), ys, segs, lpos_tok, tw_tok, g, b, tt=tt,
                   n_sorted=n_sorted)
    return out.reshape(bsz, s, D_MODEL)


def _pick_tile(n, want):
    t = min(n, want)
    while n % t:
        t //= 2
    return t


def kernel(x, mem, ev_w_in, ev_conv_a, ev_conv_b, ev_conv_b_bias, ev_w_rgate, ev_b_rgate, ev_w_igate, ev_b_igate, ev_lambda, ev_w_out, od_w_in, od_b_gates, od_norm_g, od_w_out, xa_wq, xa_wk, xa_wv, xa_wo, moe_w_router, moe_b_router, moe_w1, moe_b1, moe_w2, moe_b2, ln_g, ln_b):
    bsz, s, _ = x.shape
    ts_even = _pick_tile(s, 512)
    ts_odd = _pick_tile(s, 256)
    ts_attn = _pick_tile(s, 512)
    n_sub_attn = 2 if s % (2 * ts_attn) == 0 else 1
    tm = 1024

    w1_all = moe_w1.reshape(DEPTH * N_EXPERTS, D_MODEL, 2 * D_FF)
    w2_all = moe_w2.reshape(DEPTH * N_EXPERTS, D_FF, D_MODEL)
    b1_all = moe_b1.reshape(DEPTH * N_EXPERTS, 1, 2 * D_FF)
    b2_all = moe_b2.reshape(DEPTH * N_EXPERTS, 1, D_MODEL)
    for layer in range(DEPTH):
        j = layer // 2
        if layer % 2 == 0:
            x = _even_layer(x, ev_w_in[j], ev_conv_a[j], ev_conv_b[j], ev_conv_b_bias[j],
                            ev_w_rgate[j], ev_b_rgate[j], ev_w_igate[j], ev_b_igate[j],
                            ev_lambda[j], ev_w_out[j], ln_g[layer, 0], ln_b[layer, 0], ts=ts_even)
        else:
            x = _odd_layer(x, od_w_in[j], od_b_gates[j], od_norm_g[j], od_w_out[j],
                           ln_g[layer, 0], ln_b[layer, 0], ts=ts_odd)
        qk_mem, vo_mem = _kv_proj(mem, xa_wk[layer], xa_wv[layer], xa_wq[layer], xa_wo[layer])
        x2, x2b, tw, lpos, meta, cnt = _attn_router(
            x, qk_mem, vo_mem, ln_g[layer, 1], ln_b[layer, 1],
            moe_w_router[layer], moe_b_router[layer], ts=ts_attn, n_sub=n_sub_attn)
        x = _moe_layer(x2, x2b, tw, lpos, meta, cnt[:, 0], w1_all, b1_all, w2_all, b2_all,
                       ln_g[layer, 2], ln_b[layer, 2], layer, tm=tm, tt=ts_attn)
    return x
```

```python
import functools

import jax
import jax.numpy as jnp
from jax import lax
from jax.experimental import pallas as pl
from jax.experimental.pallas import tpu as pltpu

F32 = jnp.float32
BF16 = jnp.bfloat16
U32 = jnp.uint32
I32 = jnp.int32

D_MODEL = 1024
DEPTH = 4
A_WIDTH = 512
B_WIDTH = 1024
B_HEADS = 8
B_HEAD_DIM = 128
LRU_C = 8.0
M_HEADS = 4
M_QK_DIM = 128
M_V_DIM = 256
M_QK = 512
M_V = 1024
M_CHUNK = 128
X_HEADS = 4
X_HEAD_DIM = 256
N_EXPERTS = 32
TOP_K = 4
D_FF = 1024
SWIGLU_LIMIT = 7.0
SWIGLU_ALPHA = 1.702
DN_ALPHA = (2 * DEPTH) ** 0.25
LN_EPS = 1e-5
RMS_EPS = 1e-6
HALF = D_MODEL // 2
SEG_ALIGN = 8
MXU_DIM = 256

VMEM_LIMIT_BYTES = 56 * 1024 * 1024


def _cparams(n_grid):
    return pltpu.CompilerParams(
        dimension_semantics=("arbitrary",) * n_grid, vmem_limit_bytes=VMEM_LIMIT_BYTES)


def _layer_norm(y, g, b):
    mu = jnp.mean(y, axis=-1, keepdims=True)
    yc = y - mu
    var = jnp.mean(yc * yc, axis=-1, keepdims=True)
    return yc * lax.rsqrt(var + LN_EPS) * g + b


def _sigmoid(x):
    return 1.0 / (1.0 + jnp.exp(-x))


def _softplus(x):
    return jnp.maximum(x, 0.0) + jnp.log1p(jnp.exp(-jnp.abs(x)))


def _gelu_tanh(x):
    return 0.5 * x * (1.0 + jnp.tanh(0.7978845608028654 * (x + 0.044715 * (x * x * x))))


def _split_bf16(x):
    hi = x.astype(BF16)
    lo = (x - hi.astype(F32)).astype(BF16)
    return hi, lo


def _dot(a, b):
    return jnp.dot(a, b, preferred_element_type=F32)


def _dot_nt(a, b):
    return lax.dot_general(a, b, (((1,), (1,)), ((), ())), preferred_element_type=F32)


def _dot_tn(a, b):
    return lax.dot_general(a, b, (((0,), (0,)), ((), ())), preferred_element_type=F32)


def _pack_rows(y):
    bits = lax.bitcast_convert_type(y.astype(BF16).astype(F32), U32)
    return (bits[:, :HALF] >> 16) | (bits[:, HALF:] & jnp.uint32(0xFFFF0000))


def _unpack_rows(w):
    lo = lax.bitcast_convert_type(w << 16, F32)
    hi = lax.bitcast_convert_type(w & jnp.uint32(0xFFFF0000), F32)
    return jnp.concatenate([lo, hi], axis=1)


def _even_kernel(x_ref, win_ref, ca_ref, cb_ref, cbb_ref, wr_ref, br_ref, wi_ref, bi_ref, lam_ref,
                 wout_ref, g_ref, b_ref, o_ref,
                 av_scr, bu_scr, a_scr, b_scr, h_scr, carry_scr, *, ts):
    @pl.when(pl.program_id(1) == 0)
    def _():
        av_scr[0:8, :] = jnp.zeros((8, A_WIDTH), F32)
        bu_scr[0:8, :] = jnp.zeros((8, B_WIDTH), F32)
        carry_scr[...] = jnp.zeros((8, B_WIDTH), F32)

    x = x_ref[0]
    z = _dot(x.astype(BF16), win_ref[...])
    a_b = z[:, 0:A_WIDTH]
    a_c = z[:, A_WIDTH:2 * A_WIDTH]
    a_x = z[:, 2 * A_WIDTH:3 * A_WIDTH]
    b_u = z[:, 3 * A_WIDTH:3 * A_WIDTH + B_WIDTH]
    b_g = z[:, 3 * A_WIDTH + B_WIDTH:]

    av_scr[8:8 + ts, :] = a_c * a_x
    ca = ca_ref[...]
    conv_a = (ca[2:3, :] * av_scr[8:8 + ts, :] + ca[1:2, :] * av_scr[7:7 + ts, :]
              + ca[0:1, :] * av_scr[6:6 + ts, :])
    y_a = a_b * conv_a
    av_scr[0:8, :] = av_scr[ts:ts + 8, :]

    bu_scr[8:8 + ts, :] = b_u
    cb = cb_ref[...]
    u = (cb[3:4, :] * bu_scr[8:8 + ts, :] + cb[2:3, :] * bu_scr[7:7 + ts, :]
         + cb[1:2, :] * bu_scr[6:6 + ts, :] + cb[0:1, :] * bu_scr[5:5 + ts, :] + cbb_ref[...])
    bu_scr[0:8, :] = bu_scr[ts:ts + 8, :]

    ub = u.astype(BF16)
    r_parts, i_parts = [], []
    for h in range(B_HEADS):
        uh = ub[:, h * B_HEAD_DIM:(h + 1) * B_HEAD_DIM]
        r_parts.append(_dot(uh, wr_ref[h]))
        i_parts.append(_dot(uh, wi_ref[h]))
    r = _sigmoid(jnp.concatenate(r_parts, axis=1) + br_ref[...])
    ig = _sigmoid(jnp.concatenate(i_parts, axis=1) + bi_ref[...])
    log_a = (-LRU_C) * r * _softplus(-lam_ref[...])
    a = jnp.exp(log_a)
    a_scr[...] = a
    b_scr[...] = jnp.sqrt(1.0 - a * a) * (ig * u)

    row8 = lax.broadcasted_iota(I32, (8, B_WIDTH), 0)

    def group(i, carry):
        r0 = pl.multiple_of(i * 8, 8)
        ga = a_scr[pl.ds(r0, 8), :]
        gb = b_scr[pl.ds(r0, 8), :]
        for d in (1, 2, 4):
            keep = row8 >= d
            a_sh = jnp.where(keep, pltpu.roll(ga, d, 0), 1.0)
            b_sh = jnp.where(keep, pltpu.roll(gb, d, 0), 0.0)
            gb = ga * b_sh + gb
            ga = ga * a_sh
        hg = gb + ga * carry
        h_scr[pl.ds(r0, 8), :] = hg
        return jnp.broadcast_to(hg[7:8, :], (8, B_WIDTH))

    carry_scr[...] = lax.fori_loop(0, ts // 8, group, carry_scr[...], unroll=4)

    y_b = _gelu_tanh(b_g) * h_scr[...]
    mix = (_dot(y_a.astype(BF16), wout_ref[0:A_WIDTH, :])
           + _dot(y_b.astype(BF16), wout_ref[A_WIDTH:, :]))
    o_ref[0] = _layer_norm(DN_ALPHA * x + mix, g_ref[...], b_ref[...])


def _even_layer(x, w_in, conv_a, conv_b, conv_b_bias, w_r, b_r, w_i, b_i, lam, w_out, g, b, *, ts):
    bsz, s, _ = x.shape
    ev_in = 3 * A_WIDTH + 2 * B_WIDTH
    const2 = lambda bi, si: (0, 0)
    const3 = lambda bi, si: (0, 0, 0)
    return pl.pallas_call(
        functools.partial(_even_kernel, ts=ts),
        grid=(bsz, s // ts),
        in_specs=[
            pl.BlockSpec((1, ts, D_MODEL), lambda bi, si: (bi, si, 0)),
            pl.BlockSpec((D_MODEL, ev_in), const2),
            pl.BlockSpec((3, A_WIDTH), const2),
            pl.BlockSpec((4, B_WIDTH), const2),
            pl.BlockSpec((1, B_WIDTH), const2),
            pl.BlockSpec((B_HEADS, B_HEAD_DIM, B_HEAD_DIM), const3),
            pl.BlockSpec((1, B_WIDTH), const2),
            pl.BlockSpec((B_HEADS, B_HEAD_DIM, B_HEAD_DIM), const3),
            pl.BlockSpec((1, B_WIDTH), const2),
            pl.BlockSpec((1, B_WIDTH), const2),
            pl.BlockSpec((A_WIDTH + B_WIDTH, D_MODEL), const2),
            pl.BlockSpec((1, D_MODEL), const2),
            pl.BlockSpec((1, D_MODEL), const2),
        ],
        out_specs=pl.BlockSpec((1, ts, D_MODEL), lambda bi, si: (bi, si, 0)),
        out_shape=jax.ShapeDtypeStruct(x.shape, F32),
        scratch_shapes=[
            pltpu.VMEM((ts + 8, A_WIDTH), F32),
            pltpu.VMEM((ts + 8, B_WIDTH), F32),
            pltpu.VMEM((ts, B_WIDTH), F32),
            pltpu.VMEM((ts, B_WIDTH), F32),
            pltpu.VMEM((ts, B_WIDTH), F32),
            pltpu.VMEM((8, B_WIDTH), F32),
        ],
        compiler_params=_cparams(2),
        name="even_mixer",
    )(x, w_in.astype(BF16), conv_a, conv_b, conv_b_bias.reshape(1, -1), w_r.astype(BF16),
      b_r.reshape(1, -1), w_i.astype(BF16), b_i.reshape(1, -1), lam.reshape(1, -1),
      w_out.astype(BF16), g.reshape(1, -1), b.reshape(1, -1))


GATE_PAD = 128
V_EXT = M_V_DIM + 128


def _odd_kernel(x_ref, win_ref, wg_ref, bgc_ref, ng_ref, wout_ref, g_ref, b_ref,
                o_ref, c_scr, m_scr, h_scr, *, ts):
    L = M_CHUNK

    @pl.when(pl.program_id(1) == 0)
    def _():
        c_scr[...] = jnp.zeros(c_scr.shape, F32)
        m_scr[...] = jnp.zeros(m_scr.shape, F32)

    x = x_ref[0]
    x_hi, x_lo = _split_bf16(x)
    z = _dot(x_hi, win_ref[...])
    q_all = (z[:, 0:M_QK] * (M_QK_DIM ** -0.5)).astype(BF16)
    k_all = z[:, M_QK:2 * M_QK].astype(BF16)
    v_all = z[:, 2 * M_QK:2 * M_QK + M_V].astype(BF16)
    o_all = z[:, 2 * M_QK + M_V:]

    wg_hi, wg_lo = _split_bf16(wg_ref[...])
    both = _dot(x_hi, jnp.concatenate([wg_hi, wg_lo], axis=1))
    gates_c = both[:, 0:GATE_PAD] + _dot(x_lo, wg_hi) + both[:, GATE_PAD:] + bgc_ref[...]
    logf_c = -_softplus(-gates_c)

    rows = lax.broadcasted_iota(I32, (L, L), 0)
    cols = lax.broadcasted_iota(I32, (L, L), 1)
    causal = rows >= cols
    tril = jnp.where(causal, 1.0, 0.0).astype(BF16)
    lane0 = jnp.where(lax.broadcasted_iota(I32, (L, 128), 1) == 0, 1.0, 0.0).astype(BF16)

    for c in range(ts // L):
        sl = slice(c * L, (c + 1) * L)
        fc_hi, fc_lo = _split_bf16(logf_c[sl, :])
        fc_lo2 = (logf_c[sl, :] - fc_hi.astype(F32) - fc_lo.astype(F32)).astype(BF16)
        bcum_c = _dot(tril, fc_hi) + _dot(tril, fc_lo) + _dot(tril, fc_lo2)
        i_minus_b = gates_c[sl, :] - pltpu.roll(bcum_c, GATE_PAD - M_HEADS, 1)
        i_minus_b_t = i_minus_b.T
        for h in range(M_HEADS):
            m_st = m_scr[h][0:1, 0:1]
            b_col = bcum_c[:, M_HEADS + h:M_HEADS + h + 1]
            i_col = gates_c[sl, h:h + 1]
            b_end = b_col[L - 1:L, :]

            d = jnp.where(causal, b_col + i_minus_b_t[h:h + 1, :], -jnp.inf)
            inter = b_col + m_st
            m_t = jnp.maximum(inter, jnp.max(d, axis=1, keepdims=True))
            p = jnp.exp(d - m_t)
            w_inter = jnp.exp(inter - m_t)

            qh = q_all[sl, h * M_QK_DIM:(h + 1) * M_QK_DIM]
            kh = k_all[sl, h * M_QK_DIM:(h + 1) * M_QK_DIM]
            v_ext = jnp.concatenate([v_all[sl, h * M_V_DIM:(h + 1) * M_V_DIM], lane0], axis=1)
            qk = (_dot_nt(qh, kh) * p).astype(BF16)
            c_ext = c_scr[h]
            nd = _dot(qk, v_ext) + w_inter * _dot(qh, c_ext.astype(BF16))
            den = nd[:, M_V_DIM:M_V_DIM + 1]
            hh = nd[:, 0:M_V_DIM] / jnp.maximum(jnp.abs(den), jnp.exp(-m_t))

            g_col = b_end - b_col + i_col
            m_new = jnp.maximum(b_end + m_st, jnp.max(g_col, axis=0, keepdims=True))
            wg_col = jnp.exp(g_col - m_new)
            decay = jnp.exp(b_end + m_st - m_new)
            kv = _dot_tn(kh, (wg_col * v_ext.astype(F32)).astype(BF16))
            c_scr[h] = decay * c_ext + kv
            m_scr[h] = jnp.broadcast_to(m_new, (8, 128))

            hn = hh * lax.rsqrt(jnp.mean(hh * hh, axis=-1, keepdims=True) + RMS_EPS)
            h_scr[sl, h * M_V_DIM:(h + 1) * M_V_DIM] = hn

    gated = _sigmoid(o_all) * (h_scr[...] * ng_ref[...])
    mix = _dot(gated.astype(BF16), wout_ref[...])
    o_ref[0] = _layer_norm(DN_ALPHA * x + mix, g_ref[...], b_ref[...])


def _odd_layer(x, w_in, b_gates, norm_g, w_out, g, b, *, ts):
    bsz, s, _ = x.shape
    n_main = 2 * M_QK + 2 * M_V
    w_main = w_in[:, :n_main].astype(BF16)
    w_gate = w_in[:, n_main:]
    wg_pad = jnp.pad(w_gate, ((0, 0), (0, GATE_PAD - 2 * M_HEADS)))
    bg_col = jnp.pad(b_gates, (0, GATE_PAD - 2 * M_HEADS)).reshape(1, GATE_PAD)
    const2 = lambda bi, si: (0, 0)
    return pl.pallas_call(
        functools.partial(_odd_kernel, ts=ts),
        grid=(bsz, s // ts),
        in_specs=[
            pl.BlockSpec((1, ts, D_MODEL), lambda bi, si: (bi, si, 0)),
            pl.BlockSpec((D_MODEL, n_main), const2),
            pl.BlockSpec((D_MODEL, GATE_PAD), const2),
            pl.BlockSpec((1, GATE_PAD), const2),
            pl.BlockSpec((1, M_V), const2),
            pl.BlockSpec((M_V, D_MODEL), const2),
            pl.BlockSpec((1, D_MODEL), const2),
            pl.BlockSpec((1, D_MODEL), const2),
        ],
        out_specs=pl.BlockSpec((1, ts, D_MODEL), lambda bi, si: (bi, si, 0)),
        out_shape=jax.ShapeDtypeStruct(x.shape, F32),
        scratch_shapes=[
            pltpu.VMEM((M_HEADS, M_QK_DIM, V_EXT), F32),
            pltpu.VMEM((M_HEADS, 8, 128), F32),
            pltpu.VMEM((ts, M_V), F32),
        ],
        compiler_params=_cparams(2),
        name="odd_mixer",
    )(x, w_main, wg_pad, bg_col, norm_g.reshape(1, -1), w_out.astype(BF16),
      g.reshape(1, -1), b.reshape(1, -1))


def _kv_kernel(mem_ref, wk_ref, wv_ref, wq_ref, wo_ref, qk_ref, vo_ref):
    n_mem = mem_ref.shape[1]
    m = mem_ref[0].astype(BF16)
    k = _dot(m, wk_ref[...]).astype(BF16)
    v = _dot(m, wv_ref[...]).astype(BF16)
    for h in range(X_HEADS):
        hs = slice(h * X_HEAD_DIM, (h + 1) * X_HEAD_DIM)
        ms = slice(h * n_mem, (h + 1) * n_mem)
        qk_ref[0, :, ms] = (_dot_nt(wq_ref[:, hs], k[:, hs]) * (X_HEAD_DIM ** -0.5)).astype(BF16)
        vo_ref[0, ms, :] = _dot(v[:, hs], wo_ref[hs, :]).astype(BF16)


def _kv_proj(mem, wk, wv, wq, wo):
    bsz, n_mem, _ = mem.shape
    const2 = lambda bi: (0, 0)
    wblk = pl.BlockSpec((D_MODEL, D_MODEL), const2)
    return pl.pallas_call(
        _kv_kernel,
        grid=(bsz,),
        in_specs=[pl.BlockSpec((1, n_mem, D_MODEL), lambda bi: (bi, 0, 0)), wblk, wblk, wblk, wblk],
        out_specs=[pl.BlockSpec((1, D_MODEL, X_HEADS * n_mem), lambda bi: (bi, 0, 0)),
                   pl.BlockSpec((1, X_HEADS * n_mem, D_MODEL), lambda bi: (bi, 0, 0))],
        out_shape=[jax.ShapeDtypeStruct((bsz, D_MODEL, X_HEADS * n_mem), BF16),
                   jax.ShapeDtypeStruct((bsz, X_HEADS * n_mem, D_MODEL), BF16)],
        compiler_params=_cparams(1),
        name="memory_kv",
    )(mem, wk.astype(BF16), wv.astype(BF16), wq.astype(BF16), wo.astype(BF16))


def _attn_tile(x, qk, vo, g_ref, b_ref, wrt_ref, brt_ref, taken):
    ts = x.shape[0]
    n_mem = qk.shape[1] // X_HEADS
    sc_all = _dot(x.astype(BF16), qk)
    probs = []
    for h in range(X_HEADS):
        sc = sc_all[:, h * n_mem:(h + 1) * n_mem]
        e = jnp.exp(sc - jnp.max(sc, axis=-1, keepdims=True))
        probs.append((e / jnp.sum(e, axis=-1, keepdims=True)).astype(BF16))
    att = _dot(jnp.concatenate(probs, axis=1), vo)
    x2 = _layer_norm(DN_ALPHA * x + att, g_ref[...], b_ref[...])

    x_hi, x_lo = _split_bf16(x2)
    w_hi, w_lo = _split_bf16(wrt_ref[...])
    both = _dot_nt(jnp.concatenate([w_hi, w_lo], axis=0), x_hi)
    logits = both[0:N_EXPERTS, :] + _dot_nt(w_hi, x_lo) + both[N_EXPERTS:, :] + brt_ref[...]
    e_iota = lax.broadcasted_iota(I32, (N_EXPERTS, ts), 0)
    onehots, vals = [], []
    for _ in range(TOP_K):
        mx = jnp.max(logits, axis=0, keepdims=True)
        sel = jnp.min(jnp.where(logits == mx, e_iota, N_EXPERTS), axis=0, keepdims=True)
        oh = e_iota == sel
        onehots.append(oh)
        vals.append(mx)
        logits = jnp.where(oh, -jnp.inf, logits)
    exps = [jnp.exp(val - vals[0]) for val in vals]
    tot = exps[0] + exps[1] + exps[2] + exps[3]
    tw = jnp.concatenate([ex / tot for ex in exps], axis=0)

    member = jnp.where(onehots[0] | onehots[1] | onehots[2] | onehots[3], 1.0, 0.0)
    t_r = lax.broadcasted_iota(I32, (ts, ts), 0)
    t_c = lax.broadcasted_iota(I32, (ts, ts), 1)
    before = jnp.where(t_r < t_c, 1.0, 0.0).astype(BF16)
    pos = _dot(member.astype(BF16), before)
    n_e = jnp.sum(member, axis=1, keepdims=True)
    q_e = jnp.floor((n_e + (SEG_ALIGN - 1.0)) * (1.0 / SEG_ALIGN))
    e_r = lax.broadcasted_iota(I32, (N_EXPERTS, N_EXPERTS), 0)
    e_c = lax.broadcasted_iota(I32, (N_EXPERTS, N_EXPERTS), 1)
    lower = jnp.where(e_c < e_r, 1.0, 0.0).astype(BF16)
    q_wide = jnp.broadcast_to(q_e, (N_EXPERTS, 128)).astype(BF16)
    lo_e = SEG_ALIGN * _dot(lower, q_wide)[:, 0:1]
    p_e = SEG_ALIGN * q_e
    base = lo_e + pos
    lpos = [jnp.sum(jnp.where(oh, base, 0.0), axis=0, keepdims=True) for oh in onehots]
    lpos = jnp.concatenate(lpos, axis=0).astype(I32)
    lane = lax.broadcasted_iota(I32, (N_EXPERTS, 128), 1)
    meta = jnp.where(lane == 0, p_e, jnp.where(lane == 1, lo_e, jnp.where(lane == 2, taken, 0.0)))
    return x2, x_hi, tw, lpos, meta.astype(I32), taken + p_e


def _attn_kernel(x_ref, qk_ref, vo_ref, g_ref, b_ref, wrt_ref, brt_ref,
                 x2_ref, x2b_ref, tw_ref, lpos_ref, meta_ref, cnt_ref, cnt_scr, *, ts, n_sub):
    @pl.when((pl.program_id(0) == 0) & (pl.program_id(1) == 0))
    def _():
        cnt_scr[...] = jnp.zeros(cnt_scr.shape, F32)

    taken = cnt_scr[...]
    for u in range(n_sub):
        rs = slice(u * ts, (u + 1) * ts)
        x2, x_hi, tw, lpos, meta, taken = _attn_tile(
            x_ref[0, rs, :], qk_ref[0], vo_ref[0], g_ref, b_ref, wrt_ref, brt_ref, taken)
        x2_ref[0, rs, :] = x2
        x2b_ref[0, rs, :] = x_hi
        tw_ref[0, :, rs] = tw
        lpos_ref[0, :, rs] = lpos
        meta_ref[u] = meta
    cnt_scr[...] = taken
    cnt_ref[...] = taken.astype(I32)


def _attn_router(x, qk_mem, vo_mem, g, b, w_router, b_router, *, ts, n_sub):
    bsz, s, _ = x.shape
    n_score = qk_mem.shape[2]
    tg = ts * n_sub
    const2 = lambda bi, si: (0, 0)
    tok = lambda bi, si: (bi, si, 0)
    lanes = lambda bi, si: (bi, 0, si)
    return pl.pallas_call(
        functools.partial(_attn_kernel, ts=ts, n_sub=n_sub),
        grid=(bsz, s // tg),
        in_specs=[
            pl.BlockSpec((1, tg, D_MODEL), tok),
            pl.BlockSpec((1, D_MODEL, n_score), lambda bi, si: (bi, 0, 0)),
            pl.BlockSpec((1, n_score, D_MODEL), lambda bi, si: (bi, 0, 0)),
            pl.BlockSpec((1, D_MODEL), const2),
            pl.BlockSpec((1, D_MODEL), const2),
            pl.BlockSpec((N_EXPERTS, D_MODEL), const2),
            pl.BlockSpec((N_EXPERTS, 1), const2),
        ],
        out_specs=[
            pl.BlockSpec((1, tg, D_MODEL), tok),
            pl.BlockSpec((1, tg, D_MODEL), tok),
            pl.BlockSpec((1, TOP_K, tg), lanes),
            pl.BlockSpec((1, TOP_K, tg), lanes),
            pl.BlockSpec((n_sub, N_EXPERTS, 128), lambda bi, si: (bi * (s // tg) + si, 0, 0)),
            pl.BlockSpec((N_EXPERTS, 128), const2),
        ],
        out_shape=[
            jax.ShapeDtypeStruct((bsz, s, D_MODEL), F32),
            jax.ShapeDtypeStruct((bsz, s, D_MODEL), BF16),
            jax.ShapeDtypeStruct((bsz, TOP_K, s), F32),
            jax.ShapeDtypeStruct((bsz, TOP_K, s), I32),
            jax.ShapeDtypeStruct((bsz * (s // ts), N_EXPERTS, 128), I32),
            jax.ShapeDtypeStruct((N_EXPERTS, 128), I32),
        ],
        scratch_shapes=[pltpu.VMEM((N_EXPERTS, 128), F32)],
        compiler_params=_cparams(2),
        name="memory_attention_router",
    )(x, qk_mem, vo_mem, g.reshape(1, -1), b.reshape(1, -1), w_router.T, b_router.reshape(-1, 1))


WMAT_BLOCK = 16
SEG_CHUNK = 128
SEG_WIDTH = 4 * N_EXPERTS


def _segment_starts(seg_ref, make_copy):
    chunk_bit = SEG_CHUNK.bit_length() - 1

    def per_expert(e, carry):
        n = seg_ref[0, 0, e]
        lo = seg_ref[0, 0, N_EXPERTS + e]
        dst = seg_ref[0, 0, 2 * N_EXPERTS + e]

        def chunk(j, c):
            off = pl.multiple_of(j * SEG_CHUNK, SEG_CHUNK)
            make_copy(pl.multiple_of(lo + off, SEG_ALIGN), pl.multiple_of(dst + off, SEG_ALIGN),
                      SEG_CHUNK).start()
            return c

        lax.fori_loop(0, n >> chunk_bit, chunk, 0)
        for bit in range(chunk_bit - 1, SEG_ALIGN.bit_length() - 2, -1):
            size = 1 << bit
            done = (n >> (bit + 1)) << (bit + 1)

            @pl.when((n & size) != 0)
            def _():
                make_copy(pl.multiple_of(lo + done, SEG_ALIGN),
                          pl.multiple_of(dst + done, SEG_ALIGN), size).start(priority=bit % 2)
        return carry

    lax.fori_loop(0, N_EXPERTS, per_expert, 0)


def _segment_waits(total, make_copy, *, n_sorted):
    for bit in range(n_sorted.bit_length() - 1, SEG_ALIGN.bit_length() - 2, -1):
        size = 1 << bit

        @pl.when((total & size) != 0)
        def _():
            make_copy(0, 0, size).wait()


def _dispatch_kernel(last_ref, nused_ref, total_ref, seg_ref, lpos_ref, x_ref, xs_hbm, sort_scr,
                     zero_scr, sem, zsem, *, tt, tm, n_tiles, n_sorted):
    @pl.when(pl.program_id(0) == 0)
    def _():
        zero_scr[...] = jnp.zeros(zero_scr.shape, U32)

        def zero_copy(row):
            return pltpu.make_async_copy(
                zero_scr, xs_hbm.at[pl.ds(pl.multiple_of(row, tm), tm), :], zsem)

        for e in range(N_EXPERTS):
            @pl.when(last_ref[e] >= 0)
            def _():
                zero_copy(last_ref[e]).start()

        def start_tail(i, carry):
            zero_copy(i * tm).start()
            return carry

        lax.fori_loop(nused_ref[0], n_tiles, start_tail, 0)

        for e in range(N_EXPERTS):
            @pl.when(last_ref[e] >= 0)
            def _():
                zero_copy(last_ref[e]).wait()

        def wait_tail(i, carry):
            zero_copy(i * tm).wait()
            return carry

        lax.fori_loop(nused_ref[0], n_tiles, wait_tail, 0)

    lp = lpos_ref[0]
    lp16 = lp.astype(jnp.int16)
    r_iota = lax.broadcasted_iota(jnp.int16, (n_sorted, tt), 0)
    one = jnp.ones((), BF16)
    perm = jnp.where(r_iota == lp16[0:1, :], one, jnp.zeros((), BF16))
    for k in range(1, TOP_K):
        perm = jnp.where(r_iota == lp16[k:k + 1, :], one, perm)
    lo_bits = lax.bitcast_convert_type(_dot(perm, x_ref[:, 0:HALF]), U32)
    hi_bits = lax.bitcast_convert_type(_dot(perm, x_ref[:, HALF:]), U32)
    i = pl.program_id(0)
    buf = lax.rem(i, 2)
    sort_scr[buf] = (lo_bits >> 16) | (hi_bits & jnp.uint32(0xFFFF0000))

    def seg_copy_on(slot):
        def seg_copy(lo, dst, size):
            return pltpu.make_async_copy(sort_scr.at[slot, pl.ds(lo, size), :],
                                         xs_hbm.at[pl.ds(dst, size), :], sem.at[slot])
        return seg_copy

    _segment_starts(seg_ref, seg_copy_on(buf))

    @pl.when(i > 0)
    def _():
        _segment_waits(total_ref[i - 1], seg_copy_on(1 - buf), n_sorted=n_sorted)

    @pl.when(i == pl.num_programs(0) - 1)
    def _():
        _segment_waits(total_ref[i], seg_copy_on(buf), n_sorted=n_sorted)


def _dispatch(x2b, lpos_tiles, segs, last_tile_row, n_used, seg_totals, n_rows, *, tt, tm,
              n_sorted):
    n_tok = x2b.shape[0]
    grid_spec = pltpu.PrefetchScalarGridSpec(
        num_scalar_prefetch=3,
        grid=(n_tok // tt,),
        in_specs=[
            pl.BlockSpec((1, 1, SEG_WIDTH), lambda i, la, nu, to: (i, 0, 0),
                         memory_space=pltpu.SMEM),
            pl.BlockSpec((1, TOP_K, tt), lambda i, la, nu, to: (i, 0, 0)),
            pl.BlockSpec((tt, D_MODEL), lambda i, la, nu, to: (i, 0)),
        ],
        out_specs=pl.BlockSpec(memory_space=pl.ANY),
        scratch_shapes=[pltpu.VMEM((2, n_sorted, HALF), U32), pltpu.VMEM((tm, HALF), U32),
                        pltpu.SemaphoreType.DMA((2,)), pltpu.SemaphoreType.DMA],
    )
    return pl.pallas_call(
        functools.partial(_dispatch_kernel, tt=tt, tm=tm, n_tiles=n_rows // tm,
                          n_sorted=n_sorted),
        grid_spec=grid_spec,
        out_shape=jax.ShapeDtypeStruct((n_rows, HALF), U32),
        compiler_params=pltpu.CompilerParams(
            dimension_semantics=("arbitrary",), has_side_effects=True,
            vmem_limit_bytes=VMEM_LIMIT_BYTES),
        name="moe_dispatch",
    )(last_tile_row, n_used, seg_totals, segs, lpos_tiles, x2b)


def _ffn_kernel(te_ref, nused_ref, xs_ref, w1_ref, b1_ref, w2_ref, b2_ref, ys_ref, w1_scr, w2_scr):
    i = pl.program_id(0)
    active = i < nused_ref[0]
    switch = (i == 0) | (te_ref[i] != te_ref[jnp.maximum(i - 1, 0)])

    def ffn(w1g, w1u, w2):
        xt = _unpack_rows(xs_ref[...]).astype(BF16)
        gate = _dot(xt, w1g) + b1_ref[0, :, 0:D_FF]
        up = _dot(xt, w1u) + b1_ref[0, :, D_FF:]
        gate = jnp.minimum(gate, SWIGLU_LIMIT)
        up = jnp.clip(up, -SWIGLU_LIMIT, SWIGLU_LIMIT)
        hid = (up + 1.0) * gate * _sigmoid(SWIGLU_ALPHA * gate)
        ys_ref[...] = _pack_rows(_dot(hid.astype(BF16), w2) + b2_ref[0])

    @pl.when(active & switch)
    def _():
        w1g = w1_ref[0, :, 0:D_FF].astype(BF16)
        w1u = w1_ref[0, :, D_FF:].astype(BF16)
        w2 = w2_ref[0].astype(BF16)
        w1_scr[:, 0:D_FF] = w1g
        w1_scr[:, D_FF:] = w1u
        w2_scr[...] = w2
        ffn(w1g, w1u, w2)

    @pl.when(active & jnp.logical_not(switch))
    def _():
        ffn(w1_scr[:, 0:D_FF], w1_scr[:, D_FF:], w2_scr[...])

    @pl.when(i >= nused_ref[0])
    def _():
        ys_ref[...] = jnp.zeros(ys_ref.shape, U32)


def _grouped_ffn(xs, tile_expert, n_used, w1, b1, w2, b2, *, tm):
    n_rows = xs.shape[0]
    n_tiles = n_rows // tm
    row_blk = lambda i, te, nu: (jnp.minimum(i, nu[0] - 1), 0)
    exp_blk = lambda i, te, nu: (te[i], 0, 0)
    grid_spec = pltpu.PrefetchScalarGridSpec(
        num_scalar_prefetch=2,
        grid=(n_tiles,),
        in_specs=[
            pl.BlockSpec((tm, HALF), row_blk),
            pl.BlockSpec((1, D_MODEL, 2 * D_FF), exp_blk),
            pl.BlockSpec((1, 1, 2 * D_FF), exp_blk),
            pl.BlockSpec((1, D_FF, D_MODEL), exp_blk),
            pl.BlockSpec((1, 1, D_MODEL), exp_blk),
        ],
        out_specs=pl.BlockSpec((tm, HALF), lambda i, te, nu: (i, 0)),
        scratch_shapes=[pltpu.VMEM((D_MODEL, 2 * D_FF), BF16), pltpu.VMEM((D_FF, D_MODEL), BF16)],
    )
    return pl.pallas_call(
        _ffn_kernel,
        grid_spec=grid_spec,
        out_shape=jax.ShapeDtypeStruct((n_rows, HALF), U32),
        compiler_params=_cparams(1),
        name="moe_grouped_ffn",
    )(tile_expert, n_used, xs, w1, b1, w2, b2)


def _combine_kernel(seg_ref, segn_ref, x_ref, lpos_ref, tw_ref, g_ref, b_ref, ys_hbm, o_ref,
                    rows_scr, sem, *, tt, n_sorted):
    i = pl.program_id(0)
    buf = lax.rem(i, 2)

    def seg_copy_on(slot):
        def seg_copy(lo, src, size):
            return pltpu.make_async_copy(ys_hbm.at[pl.ds(src, size), :],
                                         rows_scr.at[slot, pl.ds(lo, size), :], sem.at[slot])
        return seg_copy

    @pl.when(i == 0)
    def _():
        rows_scr[...] = jnp.zeros(rows_scr.shape, U32)
        _segment_starts(seg_ref, seg_copy_on(0))

    @pl.when(i + 1 < pl.num_programs(0))
    def _():
        _segment_starts(segn_ref, seg_copy_on(1 - buf))

    lp = lpos_ref[...].astype(jnp.int16)
    tw = tw_ref[...].astype(BF16)
    c_iota = lax.broadcasted_iota(jnp.int16, (WMAT_BLOCK, n_sorted), 1)
    blocks = []
    for t0 in range(0, tt, WMAT_BLOCK):
        lp0 = lp[t0:t0 + WMAT_BLOCK, :]
        tw0 = tw[t0:t0 + WMAT_BLOCK, :]
        blk = jnp.where(c_iota == lp0[:, 0:1], tw0[:, 0:1], jnp.zeros((), BF16))
        for k in range(1, TOP_K):
            blk = jnp.where(c_iota == lp0[:, k:k + 1], tw0[:, k:k + 1], blk)
        blocks.append(blk)
    wmat = jnp.concatenate(blocks, axis=0)

    _segment_waits(seg_ref[0, 0, 3 * N_EXPERTS], seg_copy_on(buf), n_sorted=n_sorted)

    rows = rows_scr[buf]
    r_lo = lax.bitcast_convert_type(rows << 16, F32).astype(BF16)
    r_hi = lax.bitcast_convert_type(rows & jnp.uint32(0xFFFF0000), F32).astype(BF16)
    y = jnp.concatenate([_dot(wmat, r_lo), _dot(wmat, r_hi)], axis=1)
    o_ref[...] = _layer_norm(DN_ALPHA * x_ref[...] + y, g_ref[...], b_ref[...])


def _combine(x2, ys, segs, lpos_tok, tw_tok, g, b, *, tt, n_sorted):
    n_tok = x2.shape[0]
    const2 = lambda i: (0, 0)
    return pl.pallas_call(
        functools.partial(_combine_kernel, tt=tt, n_sorted=n_sorted),
        grid=(n_tok // tt,),
        in_specs=[
            pl.BlockSpec((1, 1, SEG_WIDTH), lambda i: (i, 0, 0), memory_space=pltpu.SMEM),
            pl.BlockSpec((1, 1, SEG_WIDTH), lambda i: (jnp.minimum(i + 1, n_tok // tt - 1), 0, 0),
                         memory_space=pltpu.SMEM),
            pl.BlockSpec((tt, D_MODEL), lambda i: (i, 0)),
            pl.BlockSpec((tt, TOP_K), lambda i: (i, 0)),
            pl.BlockSpec((tt, TOP_K), lambda i: (i, 0)),
            pl.BlockSpec((1, D_MODEL), const2),
            pl.BlockSpec((1, D_MODEL), const2),
            pl.BlockSpec(memory_space=pl.ANY),
        ],
        out_specs=pl.BlockSpec((tt, D_MODEL), lambda i: (i, 0)),
        out_shape=jax.ShapeDtypeStruct((n_tok, D_MODEL), F32),
        scratch_shapes=[pltpu.VMEM((2, n_sorted, HALF), U32), pltpu.SemaphoreType.DMA((2,))],
        compiler_params=_cparams(1),
        name="moe_combine",
    )(segs, segs, x2, lpos_tok, tw_tok, g.reshape(1, -1), b.reshape(1, -1), ys)


def _moe_layer(x2, x2b, tw, lpos, meta, counts, w1, b1, w2, b2, g, b, layer, *, tm, tt):
    bsz, s, _ = x2.shape
    n_tok = bsz * s
    n_tok_tiles = n_tok // tt
    seg_pad = N_EXPERTS * (SEG_ALIGN - 1)
    n_sorted = -(-(TOP_K * tt + seg_pad) // MXU_DIM) * MXU_DIM
    n_tiles = -(-(n_tok * TOP_K + n_tok_tiles * seg_pad) // tm) + N_EXPERTS
    n_rows = n_tiles * tm

    tiles_e = (counts + tm - 1) // tm
    tile_end = jnp.cumsum(tiles_e)
    start_e = (tile_end - tiles_e) * tm
    n_used = tile_end[-1]
    last_tile_row = jnp.where(tiles_e > 0, (tile_end - 1) * tm, -1).astype(I32)
    n_used_arr = n_used.reshape(1).astype(I32)
    tile_ids = jnp.minimum(jnp.arange(n_tiles, dtype=I32), n_used - 1)
    tile_expert = jnp.sum(tile_ids[:, None] >= tile_end[None, :], axis=1).astype(I32)
    seg_len, seg_lo = meta[:, :, 0], meta[:, :, 1]
    seg_total = jnp.broadcast_to(seg_lo[:, -1:] + seg_len[:, -1:], (n_tok_tiles, N_EXPERTS))
    segs = jnp.concatenate([seg_len, seg_lo, meta[:, :, 2] + start_e[None, :], seg_total], axis=1)
    segs = segs.reshape(n_tok_tiles, 1, SEG_WIDTH).astype(I32)
    lpos_tiles = lpos.reshape(bsz, TOP_K, s // tt, tt).transpose(0, 2, 1, 3)
    lpos_tiles = lpos_tiles.reshape(n_tok_tiles, TOP_K, tt)

    xs = _dispatch(x2b.reshape(n_tok, D_MODEL), lpos_tiles, segs, last_tile_row, n_used_arr,
                   seg_total[:, 0].astype(I32), n_rows, tt=tt, tm=tm, n_sorted=n_sorted)
    ys = _grouped_ffn(xs, tile_expert + layer * N_EXPERTS, n_used_arr, w1, b1, w2, b2, tm=tm)
    lpos_tok = lpos.transpose(0, 2, 1).reshape(n_tok, TOP_K)
    tw_tok = tw.transpose(0, 2, 1).reshape(n_tok, TOP_K)
    out = _combine(x2.reshape(n_tok, D_MODEL), ys, segs, lpos_tok, tw_tok, g, b, tt=tt,
                   n_sorted=n_sorted)
    return out.reshape(bsz, s, D_MODEL)


def _pick_tile(n, want):
    t = min(n, want)
    while n % t:
        t //= 2
    return t


def kernel(x, mem, ev_w_in, ev_conv_a, ev_conv_b, ev_conv_b_bias, ev_w_rgate, ev_b_rgate, ev_w_igate, ev_b_igate, ev_lambda, ev_w_out, od_w_in, od_b_gates, od_norm_g, od_w_out, xa_wq, xa_wk, xa_wv, xa_wo, moe_w_router, moe_b_router, moe_w1, moe_b1, moe_w2, moe_b2, ln_g, ln_b):
    bsz, s, _ = x.shape
    ts_even = _pick_tile(s, 512)
    ts_odd = _pick_tile(s, 256)
    ts_attn = _pick_tile(s, 512)
    n_sub_attn = 2 if s % (2 * ts_attn) == 0 else 1
    tm = 1024

    w1_all = moe_w1.reshape(DEPTH * N_EXPERTS, D_MODEL, 2 * D_FF)
    w2_all = moe_w2.reshape(DEPTH * N_EXPERTS, D_FF, D_MODEL)
    b1_all = moe_b1.reshape(DEPTH * N_EXPERTS, 1, 2 * D_FF)
    b2_all = moe_b2.reshape(DEPTH * N_EXPERTS, 1, D_MODEL)
    for layer in range(DEPTH):
        j = layer // 2
        if layer % 2 == 0:
            x = _even_layer(x, ev_w_in[j], ev_conv_a[j], ev_conv_b[j], ev_conv_b_bias[j],
                            ev_w_rgate[j], ev_b_rgate[j], ev_w_igate[j], ev_b_igate[j],
                            ev_lambda[j], ev_w_out[j], ln_g[layer, 0], ln_b[layer, 0], ts=ts_even)
        else:
            x = _odd_layer(x, od_w_in[j], od_b_gates[j], od_norm_g[j], od_w_out[j],
                           ln_g[layer, 0], ln_b[layer, 0], ts=ts_odd)
        qk_mem, vo_mem = _kv_proj(mem, xa_wk[layer], xa_wv[layer], xa_wq[layer], xa_wo[layer])
        x2, x2b, tw, lpos, meta, cnt = _attn_router(
            x, qk_mem, vo_mem, ln_g[layer, 1], ln_b[layer, 1],
            moe_w_router[layer], moe_b_router[layer], ts=ts_attn, n_sub=n_sub_attn)
        x = _moe_layer(x2, x2b, tw, lpos, meta, cnt[:, 0], w1_all, b1_all, w2_all, b2_all,
                       ln_g[layer, 2], ln_b[layer, 2], layer, tm=tm, tt=ts_attn)
    return x
```
